```python
import jax, jax.numpy as jnp
from jax import lax
import numpy as np

D_MODEL = 2048
BATCH = 4
SEQ = 4096
DEPTH = 2
DEC_BATCH = 128
DEC_SEQ = 4
PAST_LEN = 16384
PAGE_SIZE = 128

SB_HEADS = 8
SB_KV_HEADS = 2
SB_HEAD_DIM = 128
CONV_CH = 1024
CONV_WIDTH = 31
ML_HEADS = 4
ML_DK = 128
ML_DV = 256
ML_CHUNK = 64
GATE_CAP = 15.0
MLA_HEADS = 8
MLA_Q_LORA = 512
MLA_KV_LORA = 512
MLA_NOPE = 128
MLA_ROPE = 64
MLA_V = 128
ROPE_BASE = 10000.0
D_FF = 5632
Q_BLOCK = 128
EPS = 1e-6
LN_EPS = 1e-5

N_EVEN = (DEPTH + 1) // 2
N_ODD = DEPTH // 2
EVEN_SPLITS = (SB_HEADS * SB_HEAD_DIM, SB_KV_HEADS * SB_HEAD_DIM, SB_KV_HEADS * SB_HEAD_DIM, 2 * CONV_CH)
ODD_SPLITS = (ML_HEADS * ML_DK, ML_HEADS * ML_DK, ML_HEADS * ML_DV, ML_HEADS, ML_HEADS, ML_HEADS * ML_DV,
              MLA_Q_LORA, MLA_KV_LORA, MLA_ROPE)
EVEN_MIX = SB_HEADS * SB_HEAD_DIM + CONV_CH
ODD_MIX = ML_HEADS * ML_DV + MLA_HEADS * MLA_V

kernel_name = 'hybrid_sb_conv_mlstm_mla_step'


def rms_norm(x, g):
    xf = x.astype(jnp.float32)
    y = xf * lax.rsqrt(jnp.mean(xf * xf, axis=-1, keepdims=True) + EPS)
    return (y * g.astype(jnp.float32)).astype(x.dtype)


def split_cols(h, sizes):
    idx = np.cumsum(sizes)[:-1].tolist()
    return jnp.split(h, idx, axis=-1)


def swiglu(x, w_gate, w_up, w_down):
    return (jax.nn.silu(x @ w_gate) * (x @ w_up)) @ w_down


def softcap(x):
    return GATE_CAP * jnp.tanh(x / GATE_CAP)


def rope(x, pos):
    half = x.shape[-1] // 2
    inv = ROPE_BASE ** (-jnp.arange(half, dtype=jnp.float32) / half)
    ang = pos.astype(jnp.float32)[:, None] * inv[None, :]
    shp = (pos.shape[0],) + (1,) * (x.ndim - 3) + (half,)
    cos, sin = jnp.cos(ang).reshape(shp), jnp.sin(ang).reshape(shp)
    xf = x.astype(jnp.float32)
    x1, x2 = xf[..., :half], xf[..., half:]
    return jnp.concatenate([x1 * cos - x2 * sin, x1 * sin + x2 * cos], axis=-1).astype(x.dtype)


def stick_breaking_weights(z, mask):
    log_1mb = jnp.where(mask, jax.nn.log_sigmoid(-z), 0.0)
    suffix = lax.cumsum(log_1mb, axis=z.ndim - 1, reverse=True) - log_1mb
    return jnp.where(mask, jnp.exp(jax.nn.log_sigmoid(z) + suffix), 0.0)


def sb_prompt(q, k, v):
    b, s = q.shape[:2]
    g = SB_HEADS // SB_KV_HEADS
    nb = s // Q_BLOCK
    qb = jnp.moveaxis(q.reshape(b, nb, Q_BLOCK, SB_KV_HEADS, g, SB_HEAD_DIM), 1, 0)
    k_pos = jnp.arange(s)
    scale = SB_HEAD_DIM ** -0.5

    def block(args):
        qi, blk = args
        q_pos = blk * Q_BLOCK + jnp.arange(Q_BLOCK)
        z = jnp.einsum('bqkgd,bskd->bkgqs', qi, k, preferred_element_type=jnp.float32) * scale
        a = stick_breaking_weights(z, k_pos[None, :] < q_pos[:, None])
        return jnp.einsum('bkgqs,bskd->bqkgd', a.astype(v.dtype), v)

    o = lax.map(block, (qb, jnp.arange(nb)))
    return jnp.moveaxis(o, 0, 1).reshape(b, s, SB_HEADS * SB_HEAD_DIM)


def sb_sample(q, k_new, v_new, k_past, v_past):
    n, t = q.shape[:2]
    p = k_past.shape[1]
    g = SB_HEADS // SB_KV_HEADS
    scale = SB_HEAD_DIM ** -0.5
    qg = q.reshape(n, t, SB_KV_HEADS, g, SB_HEAD_DIM)
    z = jnp.concatenate([
        jnp.einsum('bqkgd,bskd->bkgqs', qg, k_past, preferred_element_type=jnp.float32),
        jnp.einsum('bqkgd,bskd->bkgqs', qg, k_new, preferred_element_type=jnp.float32)], axis=-1) * scale
    q_pos = p + jnp.arange(t)
    k_pos = jnp.arange(p + t)
    a = stick_breaking_weights(z, k_pos[None, :] < q_pos[:, None]).astype(v_new.dtype)
    o = (jnp.einsum('bkgqs,bskd->bqkgd', a[..., :p], v_past)
         + jnp.einsum('bkgqs,bskd->bqkgd', a[..., p:], v_new))
    return o.reshape(n, t, SB_HEADS * SB_HEAD_DIM)


def conformer_conv(u, conv_w, conv_b, ln_g, ln_b, past):
    full = jnp.concatenate([past.astype(u.dtype), u], axis=1)
    y = lax.conv_general_dilated(full, conv_w[:, None, :].astype(u.dtype), (1,), 'VALID',
                                 dimension_numbers=('NWC', 'WIO', 'NWC'),
                                 feature_group_count=CONV_CH) + conv_b
    yf = y.astype(jnp.float32)
    mu = jnp.mean(yf, axis=-1, keepdims=True)
    var = jnp.mean(jnp.square(yf - mu), axis=-1, keepdims=True)
    yn = (yf - mu) * lax.rsqrt(var + LN_EPS) * ln_g.astype(jnp.float32) + ln_b.astype(jnp.float32)
    return jax.nn.silu(yn).astype(u.dtype), full[:, -(CONV_WIDTH - 1):]


def even_mix(hn, w_in, w_out, conv_w, conv_b, ln_g, ln_b, conv_past, kv_past):
    n, t, _ = hn.shape
    q, k, v, glu = split_cols(hn @ w_in, EVEN_SPLITS)
    q = q.reshape(n, t, SB_HEADS, SB_HEAD_DIM)
    k = k.reshape(n, t, SB_KV_HEADS, SB_HEAD_DIM)
    v = v.reshape(n, t, SB_KV_HEADS, SB_HEAD_DIM)
    if kv_past is None:
        o_a = sb_prompt(q, k, v)
    else:
        o_a = sb_sample(q, k, v, kv_past[0], kv_past[1])
    val, gate = jnp.split(glu, 2, axis=-1)
    o_b, conv_state = conformer_conv(val * jax.nn.sigmoid(gate), conv_w, conv_b, ln_g, ln_b, conv_past)
    y = jnp.concatenate([o_a, o_b], axis=-1) @ w_out
    return y, k, v, conv_state


def mlstm_chunk_step(carry, inp):
    c_prev, n_prev, m_prev = carry
    q, k, v, li, lf = inp
    L = q.shape[2]
    b = jnp.cumsum(lf, axis=-1)
    causal = jnp.tril(jnp.ones((L, L), dtype=bool))
    d = jnp.where(causal, b[..., :, None] - b[..., None, :] + li[..., None, :], -jnp.inf)
    inter = b + m_prev[..., None]
    m_t = jnp.maximum(inter, jnp.max(d, axis=-1))
    s = jnp.einsum('bhtd,bhsd->bhts', q, k) * jnp.exp(d - m_t[..., None])
    dec = jnp.exp(inter - m_t)
    num = dec[..., None] * jnp.einsum('bhtd,bhde->bhte', q, c_prev) + jnp.einsum('bhts,bhse->bhte', s, v)
    den = dec * jnp.einsum('bhtd,bhd->bht', q, n_prev) + jnp.sum(s, axis=-1)
    h = num / jnp.maximum(jnp.abs(den), jnp.exp(-m_t))[..., None]
    m_new = m_t[..., -1]
    dec_state = jnp.exp(inter[..., -1] - m_new)
    w_end = jnp.exp(b[..., -1:] - b + li - m_new[..., None])
    c_new = dec_state[..., None, None] * c_prev + jnp.einsum('bhs,bhsd,bhse->bhde', w_end, k, v)
    n_new = dec_state[..., None] * n_prev + jnp.einsum('bhs,bhsd->bhd', w_end, k)
    return (c_new, n_new, m_new), h


def mlstm_mixer(q, k, v, i_pre, f_pre, o_pre, b_i, b_f, norm_g, state, chunk):
    n, t = q.shape[:2]
    f32 = jnp.float32
    nc = t // chunk

    def heads(x, d):
        return x.astype(f32).reshape(n, t, ML_HEADS, d).transpose(0, 2, 1, 3)

    def chunks(x):
        return jnp.moveaxis(x.reshape((n, ML_HEADS, nc, chunk) + x.shape[3:]), 2, 0)

    qh = heads(q, ML_DK) * (ML_DK ** -0.5)
    kh = heads(k, ML_DK)
    vh = heads(v, ML_DV)
    li = softcap(i_pre.astype(f32) + b_i.astype(f32)).transpose(0, 2, 1)
    lf = jax.nn.log_sigmoid(softcap(f_pre.astype(f32) + b_f.astype(f32))).transpose(0, 2, 1)
    init = (state[0].astype(f32), state[1].astype(f32), state[2].astype(f32))
    final, hc = lax.scan(mlstm_chunk_step, init, (chunks(qh), chunks(kh), chunks(vh), chunks(li), chunks(lf)))
    h = jnp.moveaxis(hc, 0, 2).reshape(n, ML_HEADS, t, ML_DV).transpose(0, 2, 1, 3)
    h = h * lax.rsqrt(jnp.mean(h * h, axis=-1, keepdims=True) + EPS)
    h = h.reshape(n, t, ML_HEADS * ML_DV) * norm_g.astype(f32) * jax.nn.sigmoid(o_pre.astype(f32))
    return h.astype(q.dtype), final


def mla_prompt(q_nope, q_rope, ckv, kr, w_uk, w_uv):
    b, s = q_nope.shape[:2]
    nb = s // Q_BLOCK
    scale = (MLA_NOPE + MLA_ROPE) ** -0.5
    k_nope = jnp.einsum('bsc,chd->bshd', ckv, w_uk)
    v = jnp.einsum('bsc,chd->bshd', ckv, w_uv)
    qn = jnp.moveaxis(q_nope.reshape(b, nb, Q_BLOCK, MLA_HEADS, MLA_NOPE), 1, 0)
    qr = jnp.moveaxis(q_rope.reshape(b, nb, Q_BLOCK, MLA_HEADS, MLA_ROPE), 1, 0)
    k_pos = jnp.arange(s)

    def block(args):
        qn_i, qr_i, blk = args
        q_pos = blk * Q_BLOCK + jnp.arange(Q_BLOCK)
        sc = (jnp.einsum('bqhd,bshd->bhqs', qn_i, k_nope, preferred_element_type=jnp.float32)
              + jnp.einsum('bqhr,bsr->bhqs', qr_i, kr, preferred_element_type=jnp.float32)) * scale
        sc = jnp.where(k_pos[None, :] <= q_pos[:, None], sc, -jnp.inf)
        p = jax.nn.softmax(sc, axis=-1)
        return jnp.einsum('bhqs,bshd->bqhd', p.astype(v.dtype), v)

    o = lax.map(block, (qn, qr, jnp.arange(nb)))
    return jnp.moveaxis(o, 0, 1).reshape(b, s, MLA_HEADS * MLA_V)


def mla_sample(q_nope, q_rope, ckv_new, kr_new, ckv_past, kr_past, w_uk, w_uv):
    n, t = q_nope.shape[:2]
    p = ckv_past.shape[1]
    scale = (MLA_NOPE + MLA_ROPE) ** -0.5
    q_lat = jnp.einsum('bthd,chd->bthc', q_nope, w_uk)
    f32 = jnp.float32
    sc = jnp.concatenate([
        jnp.einsum('bthc,bsc->bhts', q_lat, ckv_past, preferred_element_type=f32)
        + jnp.einsum('bthr,bsr->bhts', q_rope, kr_past, preferred_element_type=f32),
        jnp.einsum('bthc,bsc->bhts', q_lat, ckv_new, preferred_element_type=f32)
        + jnp.einsum('bthr,bsr->bhts', q_rope, kr_new, preferred_element_type=f32)], axis=-1) * scale
    q_pos = p + jnp.arange(t)
    k_pos = jnp.arange(p + t)
    sc = jnp.where(k_pos[None, :] <= q_pos[:, None], sc, -jnp.inf)
    prob = jax.nn.softmax(sc, axis=-1).astype(ckv_new.dtype)
    o_lat = (jnp.einsum('bhts,bsc->bthc', prob[..., :p], ckv_past)
             + jnp.einsum('bhts,bsc->bthc', prob[..., p:], ckv_new))
    return jnp.einsum('bthc,chd->bthd', o_lat, w_uv).reshape(n, t, MLA_HEADS * MLA_V)


def odd_mix(hn, pos, w_in, w_out, b_i, b_f, ml_norm_g, q_norm_g, kv_norm_g, w_uq, w_uk, w_uv,
            ml_state, mla_past, chunk):
    n, t, _ = hn.shape
    mq, mk, mv, mi, mf, mo, cq, ckv, kr = split_cols(hn @ w_in, ODD_SPLITS)
    o_c, new_state = mlstm_mixer(mq, mk, mv, mi, mf, mo, b_i, b_f, ml_norm_g, ml_state, chunk)
    qf = (rms_norm(cq, q_norm_g) @ w_uq).reshape(n, t, MLA_HEADS, MLA_NOPE + MLA_ROPE)
    q_nope = qf[..., :MLA_NOPE]
    q_rope = rope(qf[..., MLA_NOPE:], pos)
    ckv = rms_norm(ckv, kv_norm_g)
    kr = rope(kr, pos)
    if mla_past is None:
        o_d = mla_prompt(q_nope, q_rope, ckv, kr, w_uk, w_uv)
    else:
        o_d = mla_sample(q_nope, q_rope, ckv, kr, mla_past[0], mla_past[1], w_uk, w_uv)
    y = jnp.concatenate([o_c, o_d], axis=-1) @ w_out
    return y, new_state, ckv, kr


def setup_inputs(seed: int = 0) -> dict:
    key = jax.random.key(seed)
    keys = jax.random.split(key, 64)
    counter = [0]

    def nxt():
        k = keys[counter[0]]
        counter[0] += 1
        return k

    def nrm(shape, scale=1.0):
        return jax.random.normal(nxt(), shape, jnp.float32) * scale

    def gain(shape):
        return 1.0 + 0.01 * nrm(shape)

    n_pages = PAST_LEN // PAGE_SIZE
    n_used = DEC_BATCH * n_pages
    n_pool = n_used + (n_used + 3) // 4
    ne, no = N_EVEN, N_ODD
    d = D_MODEL
    inp = {}
    inp['x_prompt'] = nrm((BATCH, SEQ, d))
    inp['x_sample'] = nrm((DEC_BATCH, DEC_SEQ, d))
    inp['cache_sb_k'] = nrm((ne, n_pool, PAGE_SIZE, SB_KV_HEADS, SB_HEAD_DIM))
    inp['cache_sb_v'] = nrm((ne, n_pool, PAGE_SIZE, SB_KV_HEADS, SB_HEAD_DIM))
    inp['state_conv'] = nrm((ne, DEC_BATCH, CONV_WIDTH - 1, CONV_CH), 0.5)
    inp['state_mlstm_C'] = nrm((no, DEC_BATCH, ML_HEADS, ML_DK, ML_DV))
    inp['state_mlstm_n'] = nrm((no, DEC_BATCH, ML_HEADS, ML_DK))
    inp['state_mlstm_m'] = nrm((no, DEC_BATCH, ML_HEADS))
    inp['cache_mla_ckv'] = nrm((no, n_pool, PAGE_SIZE, MLA_KV_LORA))
    inp['cache_mla_krope'] = nrm((no, n_pool, PAGE_SIZE, MLA_ROPE))
    inp['page_table'] = jax.random.permutation(nxt(), n_pool)[:n_used].reshape(DEC_BATCH, n_pages).astype(jnp.int32)
    inp['ffn_norm1_g'] = gain((DEPTH, d))
    inp['mix_norm_g'] = gain((DEPTH, d))
    inp['ffn_norm2_g'] = gain((DEPTH, d))
    inp['ffn1_w_gate'] = nrm((DEPTH, d, D_FF), d ** -0.5)
    inp['ffn1_w_up'] = nrm((DEPTH, d, D_FF), d ** -0.5)
    inp['ffn1_w_down'] = nrm((DEPTH, D_FF, d), D_FF ** -0.5)
    inp['ffn2_w_gate'] = nrm((DEPTH, d, D_FF), d ** -0.5)
    inp['ffn2_w_up'] = nrm((DEPTH, d, D_FF), d ** -0.5)
    inp['ffn2_w_down'] = nrm((DEPTH, D_FF, d), D_FF ** -0.5)
    inp['even_w_in'] = nrm((ne, d, sum(EVEN_SPLITS)), d ** -0.5)
    inp['even_w_out'] = nrm((ne, EVEN_MIX, d), EVEN_MIX ** -0.5)
    inp['conv_w'] = nrm((ne, CONV_WIDTH, CONV_CH), CONV_WIDTH ** -0.5)
    inp['conv_b'] = nrm((ne, CONV_CH), 0.01)
    inp['conv_ln_g'] = gain((ne, CONV_CH))
    inp['conv_ln_b'] = nrm((ne, CONV_CH), 0.01)
    inp['odd_w_in'] = nrm((no, d, sum(ODD_SPLITS)), d ** -0.5)
    inp['odd_w_out'] = nrm((no, ODD_MIX, d), ODD_MIX ** -0.5)
    inp['mlstm_b_i'] = nrm((no, ML_HEADS), 0.1)
    inp['mlstm_b_f'] = 3.0 + nrm((no, ML_HEADS), 0.1)
    inp['mlstm_norm_g'] = gain((no, ML_HEADS * ML_DV))
    inp['mla_q_norm_g'] = gain((no, MLA_Q_LORA))
    inp['mla_kv_norm_g'] = gain((no, MLA_KV_LORA))
    inp['mla_w_uq'] = nrm((no, MLA_Q_LORA, MLA_HEADS * (MLA_NOPE + MLA_ROPE)), MLA_Q_LORA ** -0.5)
    inp['mla_w_uk'] = nrm((no, MLA_KV_LORA, MLA_HEADS, MLA_NOPE), MLA_KV_LORA ** -0.5)
    inp['mla_w_uv'] = nrm((no, MLA_KV_LORA, MLA_HEADS, MLA_V), MLA_KV_LORA ** -0.5)
    inp['final_norm_g'] = gain((d,))
    return inp


def reference(x_prompt, x_sample, cache_sb_k, cache_sb_v, state_conv, state_mlstm_C, state_mlstm_n,
              state_mlstm_m, cache_mla_ckv, cache_mla_krope, page_table,
              ffn_norm1_g, mix_norm_g, ffn_norm2_g, ffn1_w_gate, ffn1_w_up, ffn1_w_down,
              ffn2_w_gate, ffn2_w_up, ffn2_w_down, even_w_in, even_w_out, conv_w, conv_b, conv_ln_g, conv_ln_b,
              odd_w_in, odd_w_out, mlstm_b_i, mlstm_b_f, mlstm_norm_g, mla_q_norm_g, mla_kv_norm_g,
              mla_w_uq, mla_w_uk, mla_w_uv, final_norm_g):
    f32 = jnp.float32
    bp, sp = x_prompt.shape[:2]
    bs, ss = x_sample.shape[:2]
    past = page_table.shape[1] * cache_sb_k.shape[2]
    pos_p = jnp.arange(sp)
    pos_s = past + jnp.arange(ss)

    def gather(pool, layer):
        rows = pool[layer, page_table]
        return rows.reshape((bs, past) + rows.shape[3:])

    def macaron(h, g, wg, wu, wd):
        return h + 0.5 * swiglu(rms_norm(h, g), wg, wu, wd)

    hp, hs = x_prompt, x_sample
    sb_k_p, sb_v_p, sb_k_s, sb_v_s, conv_p, conv_s = [], [], [], [], [], []
    mc_p, mn_p, mm_p, mc_s, mn_s, mm_s = [], [], [], [], [], []
    ckv_p, kr_p, ckv_s, kr_s = [], [], [], []
    for layer in range(DEPTH):
        hp = macaron(hp, ffn_norm1_g[layer], ffn1_w_gate[layer], ffn1_w_up[layer], ffn1_w_down[layer])
        hs = macaron(hs, ffn_norm1_g[layer], ffn1_w_gate[layer], ffn1_w_up[layer], ffn1_w_down[layer])
        if layer % 2 == 0:
            e = layer // 2
            ep = (even_w_in[e], even_w_out[e], conv_w[e], conv_b[e], conv_ln_g[e], conv_ln_b[e])
            yp, kp, vp, cp = even_mix(rms_norm(hp, mix_norm_g[layer]), *ep,
                                      jnp.zeros((bp, CONV_WIDTH - 1, CONV_CH), hp.dtype), None)
            ys, kn, vn, cn = even_mix(rms_norm(hs, mix_norm_g[layer]), *ep, state_conv[e],
                                      (gather(cache_sb_k, e), gather(cache_sb_v, e)))
            sb_k_p.append(kp)
            sb_v_p.append(vp)
            sb_k_s.append(kn)
            sb_v_s.append(vn)
            conv_p.append(cp)
            conv_s.append(cn)
        else:
            o = layer // 2
            op = (odd_w_in[o], odd_w_out[o], mlstm_b_i[o], mlstm_b_f[o], mlstm_norm_g[o],
                  mla_q_norm_g[o], mla_kv_norm_g[o], mla_w_uq[o], mla_w_uk[o], mla_w_uv[o])
            zero_state = (jnp.zeros((bp, ML_HEADS, ML_DK, ML_DV), f32), jnp.zeros((bp, ML_HEADS, ML_DK), f32),
                          jnp.zeros((bp, ML_HEADS), f32))
            yp, st_p, ckv1, kr1 = odd_mix(rms_norm(hp, mix_norm_g[layer]), pos_p, *op, zero_state, None,
                                          min(ML_CHUNK, sp))
            ys, st_s, ckv2, kr2 = odd_mix(rms_norm(hs, mix_norm_g[layer]), pos_s, *op,
                                          (state_mlstm_C[o], state_mlstm_n[o], state_mlstm_m[o]),
                                          (gather(cache_mla_ckv, o), gather(cache_mla_krope, o)), ss)
            mc_p.append(st_p[0])
            mn_p.append(st_p[1])
            mm_p.append(st_p[2])
            mc_s.append(st_s[0])
            mn_s.append(st_s[1])
            mm_s.append(st_s[2])
            ckv_p.append(ckv1)
            kr_p.append(kr1)
            ckv_s.append(ckv2)
            kr_s.append(kr2)
        hp = hp + yp
        hs = hs + ys
        hp = macaron(hp, ffn_norm2_g[layer], ffn2_w_gate[layer], ffn2_w_up[layer], ffn2_w_down[layer])
        hs = macaron(hs, ffn_norm2_g[layer], ffn2_w_gate[layer], ffn2_w_up[layer], ffn2_w_down[layer])
    y_prompt = rms_norm(hp, final_norm_g)
    y_sample = rms_norm(hs, final_norm_g)
    return (y_prompt, y_sample,
            jnp.stack(sb_k_p), jnp.stack(sb_v_p), jnp.stack(sb_k_s), jnp.stack(sb_v_s),
            jnp.stack(conv_p), jnp.stack(conv_s),
            jnp.stack(mc_p), jnp.stack(mn_p), jnp.stack(mm_p),
            jnp.stack(mc_s), jnp.stack(mn_s), jnp.stack(mm_s),
            jnp.stack(ckv_p), jnp.stack(kr_p), jnp.stack(ckv_s), jnp.stack(kr_s))
```

```python
import functools

import numpy as np
import jax
import jax.numpy as jnp
from jax import lax
from jax.experimental import pallas as pl
from jax.experimental.pallas import tpu as pltpu

F32 = jnp.float32
BF16 = jnp.bfloat16

SB_HEADS = 8
SB_KV_HEADS = 2
SB_HEAD_DIM = 128
SB_GROUP = SB_HEADS // SB_KV_HEADS
CONV_CH = 1024
CONV_WIDTH = 31
ML_HEADS = 4
ML_DK = 128
ML_DV = 256
GATE_CAP = 15.0
MLA_HEADS = 8
MLA_Q_LORA = 512
MLA_KV_LORA = 512
MLA_NOPE = 128
MLA_ROPE = 64
MLA_V = 128
ROPE_BASE = 10000.0
EPS = 1e-6
LN_EPS = 1e-5

LANE = 128
SUBLANE = 8
VMEM_LIMIT = 56 * 1024 * 1024

NEG_BIG = -1e30

EV_Q = 0
EV_VAL = EV_Q + SB_HEADS * SB_HEAD_DIM
EV_GATE = EV_VAL + CONV_CH
EV_K = EV_GATE + CONV_CH
EV_V = EV_K + SB_KV_HEADS * SB_HEAD_DIM
EV_W = EV_V + SB_KV_HEADS * SB_HEAD_DIM

OD_Q = 0
OD_K = OD_Q + ML_HEADS * ML_DK
OD_V = OD_K + ML_HEADS * ML_DK
OD_O = OD_V + ML_HEADS * ML_DV
OD_CQ = OD_O + ML_HEADS * ML_DV
OD_CKV = OD_CQ + MLA_Q_LORA
OD_KR = OD_CKV + MLA_KV_LORA
OD_G = OD_KR + 2 * MLA_ROPE
OD_W = OD_G + LANE

QF_NOPE = 0
QF_ROPE = MLA_HEADS * MLA_NOPE
QF_ROPE_SW = QF_ROPE + MLA_HEADS * MLA_ROPE
QF_W = QF_ROPE_SW + MLA_HEADS * MLA_ROPE


def _cparams(*sem):
    return pltpu.CompilerParams(dimension_semantics=sem, vmem_limit_bytes=VMEM_LIMIT)


def _pick(n, cands):
    for c in cands:
        if n % c == 0:
            return c
    return n


def _dot(a, b):
    return jnp.dot(a, b, preferred_element_type=F32)


def _dot_nt(a, b):
    return lax.dot_general(a, b, (((1,), (1,)), ((), ())), preferred_element_type=F32)


def _rms(x, g):
    return x * lax.rsqrt(jnp.mean(x * x, axis=-1, keepdims=True) + EPS) * g


def _log_sigmoid(x):
    return jnp.minimum(x, 0.0) - jnp.log1p(jnp.exp(-jnp.abs(x)))


def _ffn_kernel(x_ref, g_ref, wg_ref, wu_ref, wd_ref, o_ref, xn_ref, acc_ref):
    j = pl.program_id(1)

    @pl.when(j == 0)
    def _():
        xn_ref[...] = _rms(x_ref[...], g_ref[...]).astype(BF16)
        acc_ref[...] = jnp.zeros_like(acc_ref)

    xn = xn_ref[...]
    h = _dot(xn, wg_ref[...])
    u = _dot(xn, wu_ref[...])
    a = (h * jax.nn.sigmoid(h) * u).astype(BF16)
    acc_ref[...] += _dot(a, wd_ref[...])

    @pl.when(j == pl.num_programs(1) - 1)
    def _():
        o_ref[...] = x_ref[...] + 0.5 * acc_ref[...]


def ffn(x, g, wg, wu, wd):
    t, d = x.shape
    f = wg.shape[1]
    tm = _pick(t, (512, 256, 128, 64, 32, 16, 8))
    tf = _pick(f, (512, 256, 128))
    return pl.pallas_call(
        _ffn_kernel,
        grid=(t // tm, f // tf),
        in_specs=[
            pl.BlockSpec((tm, d), lambda i, j: (i, 0)),
            pl.BlockSpec((1, d), lambda i, j: (0, 0)),
            pl.BlockSpec((d, tf), lambda i, j: (0, j)),
            pl.BlockSpec((d, tf), lambda i, j: (0, j)),
            pl.BlockSpec((tf, d), lambda i, j: (j, 0)),
        ],
        out_specs=pl.BlockSpec((tm, d), lambda i, j: (i, 0)),
        out_shape=jax.ShapeDtypeStruct((t, d), F32),
        scratch_shapes=[pltpu.VMEM((tm, d), BF16), pltpu.VMEM((tm, d), F32)],
        compiler_params=_cparams("parallel", "arbitrary"),
        name="ffn",
    )(x, g.reshape(1, d), wg, wu, wd)


def _nmm_kernel(x_ref, g_ref, w_ref, o_ref, xn_ref):
    @pl.when(pl.program_id(1) == 0)
    def _():
        xn_ref[...] = _rms(x_ref[...], g_ref[...]).astype(BF16)

    o_ref[...] = _dot(xn_ref[...], w_ref[...])


def norm_matmul(x, g, w, col_block=0):
    t = x.shape[0]
    k, n = w.shape
    tm = _pick(t, (512, 256, 128, 64, 32, 16, 8))
    tn = _pick(n, (512, 256, 128))
    return pl.pallas_call(
        _nmm_kernel,
        grid=(t // tm, n // tn),
        in_specs=[
            pl.BlockSpec((tm, k), lambda i, j: (i, col_block)),
            pl.BlockSpec((1, k), lambda i, j: (0, 0)),
            pl.BlockSpec((k, tn), lambda i, j: (0, j)),
        ],
        out_specs=pl.BlockSpec((tm, tn), lambda i, j: (i, j)),
        out_shape=jax.ShapeDtypeStruct((t, n), F32),
        scratch_shapes=[pltpu.VMEM((tm, k), BF16)],
        compiler_params=_cparams("parallel", "arbitrary"),
        name="norm_matmul",
    )(x, g.reshape(1, k), w)


def _mm2_kernel(a_ref, b_ref, w1_ref, w2_ref, r_ref, o_ref):
    o_ref[...] = (r_ref[...] + _dot(a_ref[...].astype(BF16), w1_ref[...])
                  + _dot(b_ref[...].astype(BF16), w2_ref[...]))


def out_proj(a, b, w1, w2, res):
    t, ka = a.shape
    kb = b.shape[1]
    n = w1.shape[1]
    tm = _pick(t, (256, 128, 64, 32, 16, 8))
    return pl.pallas_call(
        _mm2_kernel,
        grid=(t // tm,),
        in_specs=[
            pl.BlockSpec((tm, ka), lambda i: (i, 0)),
            pl.BlockSpec((tm, kb), lambda i: (i, 0)),
            pl.BlockSpec((ka, n), lambda i: (0, 0)),
            pl.BlockSpec((kb, n), lambda i: (0, 0)),
            pl.BlockSpec((tm, n), lambda i: (i, 0)),
        ],
        out_specs=pl.BlockSpec((tm, n), lambda i: (i, 0)),
        out_shape=jax.ShapeDtypeStruct((t, n), F32),
        compiler_params=_cparams("parallel"),
        name="out_proj",
    )(a, b, w1, w2, res)


def _norm_kernel(x_ref, g_ref, o_ref):
    o_ref[...] = _rms(x_ref[...], g_ref[...])


def final_norm(x, g):
    t, d = x.shape
    tm = _pick(t, (512, 256, 128, 64, 32, 16, 8))
    return pl.pallas_call(
        _norm_kernel,
        grid=(t // tm,),
        in_specs=[pl.BlockSpec((tm, d), lambda i: (i, 0)), pl.BlockSpec((1, d), lambda i: (0, 0))],
        out_specs=pl.BlockSpec((tm, d), lambda i: (i, 0)),
        out_shape=jax.ShapeDtypeStruct((t, d), F32),
        compiler_params=_cparams("parallel"),
        name="final_norm",
    )(x, g.reshape(1, d))


def _bmm_kernel(x_ref, w_ref, o_ref):
    o_ref[...] = _dot(x_ref[...].astype(BF16), w_ref[...])


def head_matmul(x, w):
    h, m, k = x.shape
    n = w.shape[2]
    return pl.pallas_call(
        _bmm_kernel,
        grid=(h,),
        in_specs=[pl.BlockSpec((None, m, k), lambda i: (i, 0, 0)),
                  pl.BlockSpec((None, k, n), lambda i: (i, 0, 0))],
        out_specs=pl.BlockSpec((None, m, n), lambda i: (i, 0, 0)),
        out_shape=jax.ShapeDtypeStruct((h, m, n), F32),
        compiler_params=_cparams("parallel"),
        name="head_matmul",
    )(x, w)


def _tri_schedule(nq, reverse):
    qs, ks = [], []
    for qi in range(nq):
        order = range(qi, -1, -1) if reverse else range(qi + 1)
        for kj in order:
            qs.append(qi)
            ks.append(kj)
    return jnp.asarray(np.array(qs, np.int32)), jnp.asarray(np.array(ks, np.int32))


def _strict_suffix_matrix(n):
    j = np.arange(n)[:, None]
    s = np.arange(n)[None, :]
    return jnp.asarray((j > s).astype(np.float32), dtype=BF16)


def _split_bf16(x):
    hi = x.astype(BF16)
    lo = (x - hi.astype(F32)).astype(BF16)
    return hi, lo


def _sbp_kernel(qt_ref, kt_ref, q_ref, k_ref, v_ref, tri_ref, o_ref, qb_ref, acc_ref, car_ref, *, tb, scale):
    step = pl.program_id(2)
    qi = qt_ref[step]
    kj = kt_ref[step]
    nchunk = tb // LANE
    tri = tri_ref[...]

    def sweep(diag):
        kb = k_ref[...].astype(BF16)
        vb = v_ref[...].astype(BF16)
        if diag:
            row = lax.broadcasted_iota(jnp.int32, (tb, LANE), 0)
            col = lax.broadcasted_iota(jnp.int32, (tb, LANE), 1)
        for c in reversed(range(nchunk)):
            kc = kb[c * LANE:(c + 1) * LANE]
            vc = vb[c * LANE:(c + 1) * LANE]
            if diag:
                mask = (col + c * LANE) < row
            for gi in range(SB_GROUP):
                z = _dot_nt(qb_ref[:, gi * SB_HEAD_DIM:(gi + 1) * SB_HEAD_DIM], kc) * scale
                l1 = _log_sigmoid(-z)
                l1m = jnp.where(mask, l1, 0.0) if diag else l1
                hi, lo = _split_bf16(l1m)
                suffix = _dot(hi, tri) + _dot(lo, tri) + car_ref[gi]
                w = jnp.exp(z + l1 + suffix)
                if diag:
                    w = jnp.where(mask, w, 0.0)
                acc_ref[gi] += _dot(w.astype(BF16), vc)
                car_ref[gi] += jnp.sum(l1m, axis=-1, keepdims=True)

    @pl.when(kj == qi)
    def _():
        qb_ref[...] = q_ref[...].astype(BF16)
        acc_ref[...] = jnp.zeros_like(acc_ref)
        car_ref[...] = jnp.zeros_like(car_ref)
        sweep(True)

    @pl.when(kj < qi)
    def _():
        sweep(False)

    @pl.when(kj == 0)
    def _():
        for gi in range(SB_GROUP):
            o_ref[:, gi * SB_HEAD_DIM:(gi + 1) * SB_HEAD_DIM] = acc_ref[gi]


def sb_prompt(proj, b, s):
    tb = _pick(s, (256, 128))
    nq = s // tb
    qt, kt = _tri_schedule(nq, reverse=True)
    gw = SB_GROUP * SB_HEAD_DIM
    kcol = EV_K // SB_HEAD_DIM
    vcol = EV_V // SB_HEAD_DIM
    grid_spec = pltpu.PrefetchScalarGridSpec(
        num_scalar_prefetch=2,
        grid=(b, SB_KV_HEADS, int(qt.shape[0])),
        in_specs=[
            pl.BlockSpec((tb, gw), lambda bi, kv, st, qt, kt: (bi * nq + qt[st], kv)),
            pl.BlockSpec((tb, SB_HEAD_DIM), lambda bi, kv, st, qt, kt: (bi * nq + kt[st], kcol + kv)),
            pl.BlockSpec((tb, SB_HEAD_DIM), lambda bi, kv, st, qt, kt: (bi * nq + kt[st], vcol + kv)),
            pl.BlockSpec((LANE, LANE), lambda bi, kv, st, qt, kt: (0, 0)),
        ],
        out_specs=pl.BlockSpec((tb, gw), lambda bi, kv, st, qt, kt: (bi * nq + qt[st], kv)),
        scratch_shapes=[pltpu.VMEM((tb, gw), BF16),
                        pltpu.VMEM((SB_GROUP, tb, SB_HEAD_DIM), F32),
                        pltpu.VMEM((SB_GROUP, tb, 1), F32)],
    )
    return pl.pallas_call(
        functools.partial(_sbp_kernel, tb=tb, scale=SB_HEAD_DIM ** -0.5),
        grid_spec=grid_spec,
        out_shape=jax.ShapeDtypeStruct((b * s, SB_HEADS * SB_HEAD_DIM), F32),
        compiler_params=_cparams("parallel", "parallel", "arbitrary"),
        name="sb_prompt",
    )(qt, kt, proj, proj, proj, _strict_suffix_matrix(LANE))


def _sbs_kernel(pt_ref, q_ref, kn_ref, vn_ref, tri_ref, *rest, pg, scale):
    k_refs = rest[:pg]
    v_refs = rest[pg:2 * pg]
    o_ref, acc_ref, car_ref = rest[2 * pg:]
    g = pl.program_id(1)
    rows, width = q_ref.shape[0], kn_ref.shape[0]
    qb = q_ref[...].astype(BF16)
    row = lax.broadcasted_iota(jnp.int32, (rows, width), 0)
    col = lax.broadcasted_iota(jnp.int32, (rows, width), 1)
    tokens = rows // (SB_KV_HEADS * SB_GROUP)
    row_kv = row // (SB_GROUP * tokens)
    row_t = row % tokens
    own = (col % SB_KV_HEADS) == row_kv
    tri = tri_ref[...]

    def page(k_ref, v_ref, mask):
        z = _dot_nt(qb, k_ref[...].astype(BF16)) * scale
        l1 = _log_sigmoid(-z)
        l1m = jnp.where(mask, l1, 0.0)
        hi, lo = _split_bf16(l1m)
        suffix = _dot(hi, tri) + _dot(lo, tri) + car_ref[...]
        w = jnp.where(mask, jnp.exp(z + l1 + suffix), 0.0)
        acc_ref[...] += _dot(w.astype(BF16), v_ref[...].astype(BF16))
        car_ref[...] += jnp.sum(l1m, axis=-1, keepdims=True)

    @pl.when(g == 0)
    def _():
        acc_ref[...] = jnp.zeros_like(acc_ref)
        car_ref[...] = jnp.zeros_like(car_ref)
        page(kn_ref, vn_ref, own & ((col // SB_KV_HEADS) < row_t))

    for i in reversed(range(pg)):
        page(k_refs[i], v_refs[i], own)

    @pl.when(g == pl.num_programs(1) - 1)
    def _():
        o_ref[...] = acc_ref[...]


def sb_sample(q, k_new, v_new, k_pool, v_pool, page_table, page_size):
    n, rows, dh = q.shape
    n_pages = page_table.shape[1]
    pw = page_size * SB_KV_HEADS
    pg = _pick(n_pages, (8, 4, 2, 1))
    ng = n_pages // pg

    def page_map(i):
        return lambda ni, g, pt: (pt[ni * n_pages + (ng - 1 - g) * pg + i], 0)

    pool_specs = [pl.BlockSpec((pw, dh), page_map(i)) for i in range(pg)]
    grid_spec = pltpu.PrefetchScalarGridSpec(
        num_scalar_prefetch=1,
        grid=(n, ng),
        in_specs=[
            pl.BlockSpec((None, rows, dh), lambda ni, g, pt: (ni, 0, 0)),
            pl.BlockSpec((None, pw, dh), lambda ni, g, pt: (ni, 0, 0)),
            pl.BlockSpec((None, pw, dh), lambda ni, g, pt: (ni, 0, 0)),
            pl.BlockSpec((pw, pw), lambda ni, g, pt: (0, 0)),
        ] + pool_specs + pool_specs,
        out_specs=pl.BlockSpec((None, rows, dh), lambda ni, g, pt: (ni, 0, 0)),
        scratch_shapes=[pltpu.VMEM((rows, dh), F32), pltpu.VMEM((rows, 1), F32)],
    )
    return pl.pallas_call(
        functools.partial(_sbs_kernel, pg=pg, scale=dh ** -0.5),
        grid_spec=grid_spec,
        out_shape=jax.ShapeDtypeStruct((n, rows, dh), F32),
        compiler_params=_cparams("parallel", "arbitrary"),
        name="sb_sample",
    )(page_table.reshape(-1), q, k_new, v_new, _strict_suffix_matrix(pw),
      *([k_pool] * pg), *([v_pool] * pg))


def _conv_kernel(*refs, tt, rb, has_past):
    if has_past:
        val_ref, gate_ref, past_ref, cw_ref, cb_ref, lg_ref, lb_ref, o_ref, st_ref, full_ref = refs
    else:
        val_ref, gate_ref, cw_ref, cb_ref, lg_ref, lb_ref, o_ref, st_ref, full_ref = refs
    i = pl.program_id(1)
    hist = CONV_WIDTH - 1
    base = 32 - hist

    @pl.when(i == 0)
    def _():
        full_ref[0:32, :] = jnp.zeros((32, CONV_CH), F32)
        if has_past:
            full_ref[base:32, :] = past_ref[...]

    full_ref[32:32 + tt, :] = val_ref[...] * jax.nn.sigmoid(gate_ref[...])

    for r0 in range(0, tt, rb):
        ys = []
        for c0 in range(0, CONV_CH, LANE):
            acc = jnp.zeros((rb, LANE), F32)
            for w in range(CONV_WIDTH):
                acc = acc + full_ref[r0 + base + w:r0 + base + w + rb, c0:c0 + LANE] * cw_ref[w:w + 1, c0:c0 + LANE]
            ys.append(acc)
        y = jnp.concatenate(ys, axis=-1) + cb_ref[...]
        mu = jnp.mean(y, axis=-1, keepdims=True)
        var = jnp.mean(jnp.square(y - mu), axis=-1, keepdims=True)
        yn = (y - mu) * lax.rsqrt(var + LN_EPS) * lg_ref[...] + lb_ref[...]
        o_ref[r0:r0 + rb, :] = yn * jax.nn.sigmoid(yn)

    @pl.when(i == pl.num_programs(1) - 1)
    def _():
        st_ref[...] = full_ref[tt + base:tt + 32, :]

    @pl.when(i < pl.num_programs(1) - 1)
    def _():
        full_ref[0:32, :] = full_ref[tt:tt + 32, :]


def conformer_conv(proj3, past, cw, cb, lg, lb):
    b, t, _ = proj3.shape
    tt = _pick(t, (128, 64, 32, 16, 8))
    rb = _pick(tt, (64, 32, 16, 8))
    c = CONV_CH
    hist = CONV_WIDTH - 1
    vcol, gcol = EV_VAL // c, EV_GATE // c
    in_specs = [pl.BlockSpec((None, tt, c), lambda bi, i: (bi, i, vcol)),
                pl.BlockSpec((None, tt, c), lambda bi, i: (bi, i, gcol))]
    args = [proj3, proj3]
    if past is not None:
        in_specs.append(pl.BlockSpec((None, hist, c), lambda bi, i: (bi, 0, 0)))
        args.append(past)
    in_specs += [pl.BlockSpec((CONV_WIDTH, c), lambda bi, i: (0, 0))] + [pl.BlockSpec((1, c), lambda bi, i: (0, 0))] * 3
    args += [cw, cb.reshape(1, c), lg.reshape(1, c), lb.reshape(1, c)]
    return pl.pallas_call(
        functools.partial(_conv_kernel, tt=tt, rb=rb, has_past=past is not None),
        grid=(b, t // tt),
        in_specs=in_specs,
        out_specs=[pl.BlockSpec((None, tt, c), lambda bi, i: (bi, i, 0)),
                   pl.BlockSpec((None, hist, c), lambda bi, i: (bi, 0, 0))],
        out_shape=[jax.ShapeDtypeStruct((b, t, c), F32), jax.ShapeDtypeStruct((b, hist, c), F32)],
        scratch_shapes=[pltpu.VMEM((32 + tt, c), F32)],
        compiler_params=_cparams("parallel", "arbitrary"),
        name="conformer_conv",
    )(*args)


def _mlstm_kernel(*refs, tb, L, zero_init):
    if zero_init:
        (q_ref, k_ref, v_ref, og_ref, gr_ref, gb_ref, ng_ref,
         h_ref, co_ref, no_ref, mo_ref, c_s, n_s, m_s, qp, kp, vp) = refs
    else:
        (q_ref, k_ref, v_ref, og_ref, gr_ref, gb_ref, ng_ref, c0_ref, n0_ref, m0_ref,
         h_ref, co_ref, no_ref, mo_ref, c_s, n_s, m_s, qp, kp, vp) = refs
    ci = pl.program_id(1)

    @pl.when(ci == 0)
    def _():
        if zero_init:
            c_s[...] = jnp.zeros_like(c_s)
            n_s[...] = jnp.zeros_like(n_s)
            m_s[...] = jnp.zeros_like(m_s)
        else:
            c_s[...] = c0_ref[...]
            n_s[...] = n0_ref[...]
            m_s[...] = m0_ref[...]

    if tb < L:
        qp[...] = jnp.zeros_like(qp)
        kp[...] = jnp.zeros_like(kp)
        vp[...] = jnp.zeros_like(vp)
        qp[0:tb, :] = q_ref[...]
        kp[0:tb, :] = k_ref[...]
        vp[0:tb, :] = v_ref[...]
        q_src, k_src, v_src = qp, kp, vp
    else:
        q_src, k_src, v_src = q_ref, k_ref, v_ref

    pre = gr_ref[...] + gb_ref[...]
    capped = GATE_CAP * jnp.tanh(pre / GATE_CAP)
    lane = lax.broadcasted_iota(jnp.int32, (2 * ML_HEADS, L), 1)
    gate_row = lax.broadcasted_iota(jnp.int32, (2 * ML_HEADS, L), 0)
    logg = jnp.where(gate_row < ML_HEADS, capped, _log_sigmoid(capped))
    if tb < L:
        logg = jnp.where(lane < tb, logg, jnp.where(gate_row < ML_HEADS, NEG_BIG, 0.0))

    rt = lax.broadcasted_iota(jnp.int32, (L, L), 0)
    cs = lax.broadcasted_iota(jnp.int32, (L, L), 1)
    eye = rt == cs
    causal = cs <= rt

    def to_col(r):
        return jnp.sum(jnp.where(eye, r, 0.0), axis=1, keepdims=True)

    for h in range(ML_HEADS):
        li_r = logg[h:h + 1, :]
        lf_r = logg[ML_HEADS + h:ML_HEADS + h + 1, :]
        lf_c = to_col(lf_r)
        b_c = jnp.sum(jnp.where(causal, lf_r, 0.0), axis=1, keepdims=True)
        b_r = jnp.sum(jnp.where(rt <= cs, lf_c, 0.0), axis=0, keepdims=True)
        d = jnp.where(causal, b_c - b_r + li_r, -jnp.inf)
        m_prev = m_s[h][:, 0:1]
        inter = b_c + m_prev
        m_c = jnp.maximum(inter, jnp.max(d, axis=1, keepdims=True))
        q = q_src[:, h * ML_DK:(h + 1) * ML_DK] * (ML_DK ** -0.5)
        k = k_src[:, h * ML_DK:(h + 1) * ML_DK]
        v = v_src[:, h * ML_DV:(h + 1) * ML_DV]
        qb, kb, vb = q.astype(BF16), k.astype(BF16), v.astype(BF16)
        sm = _dot_nt(qb, kb) * jnp.exp(d - m_c)
        dec = jnp.exp(inter - m_c)
        c_prev = c_s[h]
        n_prev = n_s[h]
        num = dec * _dot(qb, c_prev.astype(BF16)) + _dot(sm.astype(BF16), vb)
        den = dec * jnp.sum(q * n_prev, axis=1, keepdims=True) + jnp.sum(sm, axis=1, keepdims=True)
        hh = num / jnp.maximum(jnp.abs(den), jnp.exp(-m_c))
        hn = hh * lax.rsqrt(jnp.mean(hh * hh, axis=-1, keepdims=True) + EPS)
        hn = hn * ng_ref[:, h * ML_DV:(h + 1) * ML_DV]
        og = og_ref[:, h * ML_DV:(h + 1) * ML_DV]
        h_ref[:, h * ML_DV:(h + 1) * ML_DV] = hn[0:tb, :] * jax.nn.sigmoid(og)

        m_new = m_c[L - 1:L, :]
        dec_state = jnp.exp(inter[L - 1:L, :] - m_new)
        w_end_r = jnp.exp(b_r[:, L - 1:L] - b_r + li_r - m_new)
        w_end_c = to_col(w_end_r)
        kt = k.T.astype(BF16)
        c_s[h] = dec_state * c_prev + _dot(kt, (w_end_c * v).astype(BF16))
        n_s[h] = dec_state * n_prev + jnp.sum(w_end_c * k, axis=0, keepdims=True)
        m_s[h] = jnp.broadcast_to(m_new, (1, LANE))

    @pl.when(ci == pl.num_programs(1) - 1)
    def _():
        co_ref[...] = c_s[...]
        no_ref[...] = n_s[...]
        mo_ref[...] = m_s[...]


def mlstm(proj3, gates_row, gate_bias, norm_g, state, L):
    b, t, _ = proj3.shape
    nc = gates_row.shape[1]
    tb = t // nc
    hk, hv = ML_HEADS * ML_DK, ML_HEADS * ML_DV
    zero_init = state is None
    in_specs = [
        pl.BlockSpec((None, tb, hk), lambda bi, ci: (bi, ci, OD_Q // hk)),
        pl.BlockSpec((None, tb, hk), lambda bi, ci: (bi, ci, OD_K // hk)),
        pl.BlockSpec((None, tb, hv), lambda bi, ci: (bi, ci, OD_V // hv)),
        pl.BlockSpec((None, tb, hv), lambda bi, ci: (bi, ci, OD_O // hv)),
        pl.BlockSpec((None, None, 2 * ML_HEADS, L), lambda bi, ci: (bi, ci, 0, 0)),
        pl.BlockSpec((2 * ML_HEADS, 1), lambda bi, ci: (0, 0)),
        pl.BlockSpec((1, hv), lambda bi, ci: (0, 0)),
    ]
    args = [proj3, proj3, proj3, proj3, gates_row, gate_bias.reshape(2 * ML_HEADS, 1), norm_g.reshape(1, hv)]
    state_specs = [
        pl.BlockSpec((None, ML_HEADS, ML_DK, ML_DV), lambda bi, ci: (bi, 0, 0, 0)),
        pl.BlockSpec((None, ML_HEADS, 1, ML_DK), lambda bi, ci: (bi, 0, 0, 0)),
        pl.BlockSpec((None, ML_HEADS, 1, LANE), lambda bi, ci: (bi, 0, 0, 0)),
    ]
    if not zero_init:
        c0, n0, m0 = state
        in_specs += state_specs
        args += [c0, n0.reshape(b, ML_HEADS, 1, ML_DK),
                 jnp.broadcast_to(m0.reshape(b, ML_HEADS, 1, 1), (b, ML_HEADS, 1, LANE))]
    h, c, n, m = pl.pallas_call(
        functools.partial(_mlstm_kernel, tb=tb, L=L, zero_init=zero_init),
        grid=(b, nc),
        in_specs=in_specs,
        out_specs=[pl.BlockSpec((None, tb, hv), lambda bi, ci: (bi, ci, 0))] + state_specs,
        out_shape=[jax.ShapeDtypeStruct((b, t, hv), F32),
                   jax.ShapeDtypeStruct((b, ML_HEADS, ML_DK, ML_DV), F32),
                   jax.ShapeDtypeStruct((b, ML_HEADS, 1, ML_DK), F32),
                   jax.ShapeDtypeStruct((b, ML_HEADS, 1, LANE), F32)],
        scratch_shapes=[pltpu.VMEM((ML_HEADS, ML_DK, ML_DV), F32),
                        pltpu.VMEM((ML_HEADS, 1, ML_DK), F32),
                        pltpu.VMEM((ML_HEADS, 1, LANE), F32),
                        pltpu.VMEM((L, hk), F32), pltpu.VMEM((L, hk), F32), pltpu.VMEM((L, hv), F32)],
        compiler_params=_cparams("parallel", "arbitrary"),
        name="mlstm",
    )(*args)
    return h, c, n.reshape(b, ML_HEADS, ML_DK), m[:, :, 0, 0]


def _mla_prep_kernel(ckv_ref, kr_ref, qa_ref, qb_ref, cos_ref, sin_ref, g_ref, ckv_o, kr_o, qr_o):
    ckv_o[...] = _rms(ckv_ref[...], g_ref[...])
    cos = cos_ref[...]
    sin = sin_ref[...]
    lane = lax.broadcasted_iota(jnp.int32, cos.shape, 1)
    y = kr_ref[...] * jnp.where(lane < MLA_ROPE, cos, sin)
    kr_o[...] = y[:, :MLA_ROPE] + y[:, MLA_ROPE:]
    for j in range(qa_ref.shape[1] // LANE):
        sl = slice(j * LANE, (j + 1) * LANE)
        qr_o[:, sl] = qa_ref[:, sl] * cos + qb_ref[:, sl] * sin


def mla_prep(proj, qf, cos, sin, kv_norm_g):
    t = proj.shape[0]
    tm = _pick(t, (512, 256, 128, 64, 32, 16, 8))
    rw = MLA_HEADS * MLA_ROPE
    return pl.pallas_call(
        _mla_prep_kernel,
        grid=(t // tm,),
        in_specs=[
            pl.BlockSpec((tm, MLA_KV_LORA), lambda i: (i, OD_CKV // MLA_KV_LORA)),
            pl.BlockSpec((tm, LANE), lambda i: (i, OD_KR // LANE)),
            pl.BlockSpec((tm, rw), lambda i: (i, QF_ROPE // rw)),
            pl.BlockSpec((tm, rw), lambda i: (i, QF_ROPE_SW // rw)),
            pl.BlockSpec((tm, LANE), lambda i: (i, 0)),
            pl.BlockSpec((tm, LANE), lambda i: (i, 0)),
            pl.BlockSpec((1, MLA_KV_LORA), lambda i: (0, 0)),
        ],
        out_specs=[pl.BlockSpec((tm, MLA_KV_LORA), lambda i: (i, 0)),
                   pl.BlockSpec((tm, MLA_ROPE), lambda i: (i, 0)),
                   pl.BlockSpec((tm, rw), lambda i: (i, 0))],
        out_shape=[jax.ShapeDtypeStruct((t, MLA_KV_LORA), F32),
                   jax.ShapeDtypeStruct((t, MLA_ROPE), F32),
                   jax.ShapeDtypeStruct((t, rw), F32)],
        compiler_params=_cparams("parallel"),
        name="mla_prep",
    )(proj, proj, qf, qf, cos, sin, kv_norm_g.reshape(1, MLA_KV_LORA))


def _mlap_kernel(qt_ref, kt_ref, qn_ref, qr_ref, kn_ref, kr_ref, v_ref, o_ref,
                 qnb, qrb, acc_ref, m_ref, l_ref, *, tb, scale):
    step = pl.program_id(1)
    qi = qt_ref[step]
    kj = kt_ref[step]

    @pl.when(kj == 0)
    def _():
        qnb[...] = qn_ref[...].astype(BF16)
        qrb[...] = qr_ref[...].astype(BF16)
        acc_ref[...] = jnp.zeros_like(acc_ref)
        m_ref[...] = jnp.full_like(m_ref, NEG_BIG)
        l_ref[...] = jnp.zeros_like(l_ref)

    def sweep(diag):
        knb = kn_ref[...].astype(BF16)
        krb = kr_ref[...].astype(BF16)
        vb = v_ref[...].astype(BF16)
        if diag:
            row = lax.broadcasted_iota(jnp.int32, (tb, tb), 0)
            col = lax.broadcasted_iota(jnp.int32, (tb, tb), 1)
            mask = col <= row
        for h in range(MLA_HEADS):
            ns = slice(h * MLA_NOPE, (h + 1) * MLA_NOPE)
            rs = slice(h * MLA_ROPE, (h + 1) * MLA_ROPE)
            vs = slice(h * MLA_V, (h + 1) * MLA_V)
            s = (_dot_nt(qnb[:, ns], knb[:, ns]) + _dot_nt(qrb[:, rs], krb)) * scale
            if diag:
                s = jnp.where(mask, s, NEG_BIG)
            m_prev = m_ref[h]
            m_new = jnp.maximum(m_prev, jnp.max(s, axis=-1, keepdims=True))
            alpha = jnp.exp(m_prev - m_new)
            p = jnp.exp(s - m_new)
            if diag:
                p = jnp.where(mask, p, 0.0)
            l_ref[h] = alpha * l_ref[h] + jnp.sum(p, axis=-1, keepdims=True)
            acc_ref[:, vs] = alpha * acc_ref[:, vs] + _dot(p.astype(BF16), vb[:, vs])
            m_ref[h] = m_new

    @pl.when(kj < qi)
    def _():
        sweep(False)

    @pl.when(kj == qi)
    def _():
        sweep(True)
        for h in range(MLA_HEADS):
            vs = slice(h * MLA_V, (h + 1) * MLA_V)
            o_ref[:, vs] = acc_ref[:, vs] / l_ref[h]


def mla_prompt(qf, qr, kv, kr, b, s):
    tb = _pick(s, (256, 128))
    nq = s // tb
    qt, kt = _tri_schedule(nq, reverse=False)
    nw, rw, vw = MLA_HEADS * MLA_NOPE, MLA_HEADS * MLA_ROPE, MLA_HEADS * MLA_V
    grid_spec = pltpu.PrefetchScalarGridSpec(
        num_scalar_prefetch=2,
        grid=(b, int(qt.shape[0])),
        in_specs=[
            pl.BlockSpec((tb, nw), lambda bi, st, qt, kt: (bi * nq + qt[st], QF_NOPE // nw)),
            pl.BlockSpec((tb, rw), lambda bi, st, qt, kt: (bi * nq + qt[st], 0)),
            pl.BlockSpec((tb, nw), lambda bi, st, qt, kt: (bi * nq + kt[st], 0)),
            pl.BlockSpec((tb, MLA_ROPE), lambda bi, st, qt, kt: (bi * nq + kt[st], 0)),
            pl.BlockSpec((tb, vw), lambda bi, st, qt, kt: (bi * nq + kt[st], nw // vw)),
        ],
        out_specs=pl.BlockSpec((tb, vw), lambda bi, st, qt, kt: (bi * nq + qt[st], 0)),
        scratch_shapes=[pltpu.VMEM((tb, nw), BF16), pltpu.VMEM((tb, rw), BF16),
                        pltpu.VMEM((tb, vw), F32),
                        pltpu.VMEM((MLA_HEADS, tb, 1), F32), pltpu.VMEM((MLA_HEADS, tb, 1), F32)],
    )
    return pl.pallas_call(
        functools.partial(_mlap_kernel, tb=tb, scale=(MLA_NOPE + MLA_ROPE) ** -0.5),
        grid_spec=grid_spec,
        out_shape=jax.ShapeDtypeStruct((b * s, vw), F32),
        compiler_params=_cparams("parallel", "arbitrary"),
        name="mla_prompt",
    )(qt, kt, qf, qr, kv, kr, kv)


def _mlas_kernel(pt_ref, ql_ref, qr_ref, cn_ref, rn_ref, *rest, pg, scale, tokens):
    c_refs = rest[:pg]
    r_refs = rest[pg:2 * pg]
    o_ref, acc_ref, m_ref, l_ref = rest[2 * pg:]
    g = pl.program_id(1)
    rows, width = ql_ref.shape[0], cn_ref.shape[0]
    qlb = ql_ref[...].astype(BF16)
    qrb = qr_ref[...].astype(BF16)

    def page(c_ref, r_ref, mask):
        cb = c_ref[...].astype(BF16)
        s = (_dot_nt(qlb, cb) + _dot_nt(qrb, r_ref[...].astype(BF16))) * scale
        if mask is not None:
            s = jnp.where(mask, s, NEG_BIG)
        m_prev = m_ref[...]
        m_new = jnp.maximum(m_prev, jnp.max(s, axis=-1, keepdims=True))
        alpha = jnp.exp(m_prev - m_new)
        p = jnp.exp(s - m_new)
        if mask is not None:
            p = jnp.where(mask, p, 0.0)
        l_ref[...] = alpha * l_ref[...] + jnp.sum(p, axis=-1, keepdims=True)
        acc_ref[...] = alpha * acc_ref[...] + _dot(p.astype(BF16), cb)
        m_ref[...] = m_new

    @pl.when(g == 0)
    def _():
        acc_ref[...] = jnp.zeros_like(acc_ref)
        m_ref[...] = jnp.full_like(m_ref, NEG_BIG)
        l_ref[...] = jnp.zeros_like(l_ref)
        row = lax.broadcasted_iota(jnp.int32, (rows, width), 0)
        col = lax.broadcasted_iota(jnp.int32, (rows, width), 1)
        page(cn_ref, rn_ref, col <= (row % tokens))

    for i in range(pg):
        page(c_refs[i], r_refs[i], None)

    @pl.when(g == pl.num_programs(1) - 1)
    def _():
        o_ref[...] = acc_ref[...] / l_ref[...]


def mla_sample(q_lat, q_rope, ckv_new, kr_new, ckv_pool, kr_pool, page_table, tokens):
    n, rows, c = q_lat.shape
    r = q_rope.shape[2]
    page = ckv_pool.shape[1]
    n_pages = page_table.shape[1]
    pg = _pick(n_pages, (8, 4, 2, 1))
    ng = n_pages // pg

    def page_map(i):
        return lambda ni, g, pt: (pt[ni * n_pages + g * pg + i], 0, 0)

    grid_spec = pltpu.PrefetchScalarGridSpec(
        num_scalar_prefetch=1,
        grid=(n, ng),
        in_specs=[
            pl.BlockSpec((None, rows, c), lambda ni, g, pt: (ni, 0, 0)),
            pl.BlockSpec((None, rows, r), lambda ni, g, pt: (ni, 0, 0)),
            pl.BlockSpec((None, page, c), lambda ni, g, pt: (ni, 0, 0)),
            pl.BlockSpec((None, page, r), lambda ni, g, pt: (ni, 0, 0)),
        ] + [pl.BlockSpec((None, page, c), page_map(i)) for i in range(pg)]
          + [pl.BlockSpec((None, page, r), page_map(i)) for i in range(pg)],
        out_specs=pl.BlockSpec((None, rows, c), lambda ni, g, pt: (ni, 0, 0)),
        scratch_shapes=[pltpu.VMEM((rows, c), F32), pltpu.VMEM((rows, 1), F32), pltpu.VMEM((rows, 1), F32)],
    )
    return pl.pallas_call(
        functools.partial(_mlas_kernel, pg=pg, scale=(MLA_NOPE + MLA_ROPE) ** -0.5, tokens=tokens),
        grid_spec=grid_spec,
        out_shape=jax.ShapeDtypeStruct((n, rows, c), F32),
        compiler_params=_cparams("parallel", "arbitrary"),
        name="mla_sample",
    )(page_table.reshape(-1), q_lat, q_rope, ckv_new, kr_new, *([ckv_pool] * pg), *([kr_pool] * pg))


def _swap_half(w):
    half = w.shape[-1] // 2
    return jnp.concatenate([-w[..., half:], w[..., :half]], axis=-1)


def _even_w_in(w):
    sizes = (SB_HEADS * SB_HEAD_DIM, SB_KV_HEADS * SB_HEAD_DIM, SB_KV_HEADS * SB_HEAD_DIM, 2 * CONV_CH)
    q, k, v, glu = jnp.split(w, np.cumsum(sizes)[:-1].tolist(), axis=1)
    return jnp.concatenate([q, glu, k, v], axis=1).astype(BF16)


def _odd_w_in(w):
    d = w.shape[0]
    sizes = (ML_HEADS * ML_DK, ML_HEADS * ML_DK, ML_HEADS * ML_DV, ML_HEADS, ML_HEADS, ML_HEADS * ML_DV,
             MLA_Q_LORA, MLA_KV_LORA, MLA_ROPE)
    mq, mk, mv, mi, mf, mo, cq, ckv, kr = jnp.split(w, np.cumsum(sizes)[:-1].tolist(), axis=1)
    pad = jnp.zeros((d, LANE - 2 * ML_HEADS), w.dtype)
    return jnp.concatenate([mq, mk, mv, mo, cq, ckv, kr, _swap_half(kr), mi, mf, pad], axis=1).astype(BF16)


def _w_uq(w):
    k = w.shape[0]
    w3 = w.reshape(k, MLA_HEADS, MLA_NOPE + MLA_ROPE)
    nope = w3[..., :MLA_NOPE].reshape(k, -1)
    rope = w3[..., MLA_NOPE:]
    return jnp.concatenate([nope, rope.reshape(k, -1), _swap_half(rope).reshape(k, -1)], axis=1).astype(BF16)


def _rope_tables(pos):
    half = MLA_ROPE // 2
    inv = ROPE_BASE ** (-jnp.arange(half, dtype=F32) / half)
    ang = pos.astype(F32)[:, None] * inv[None, :]
    reps = LANE // half
    return jnp.tile(jnp.cos(ang), (1, reps)), jnp.tile(jnp.sin(ang), (1, reps))


def _even_layer(hp, hs, mix_g, w_in, w_out, cw, cb, lg, lb, state_conv, k_pool, v_pool, page_table,
                bp, sp, bs, ss, page_size):
    hq = SB_HEADS * SB_HEAD_DIM
    hkv = SB_KV_HEADS * SB_HEAD_DIM
    w1, w2 = w_out[:hq], w_out[hq:]

    pp = norm_matmul(hp, mix_g, w_in)
    o_a = sb_prompt(pp, bp, sp)
    o_b, conv_p = conformer_conv(pp.reshape(bp, sp, EV_W), None, cw, cb, lg, lb)
    hp = out_proj(o_a, o_b.reshape(bp * sp, CONV_CH), w1, w2, hp)
    k_p = pp[:, EV_K:EV_K + hkv].reshape(bp, sp, SB_KV_HEADS, SB_HEAD_DIM)
    v_p = pp[:, EV_V:EV_V + hkv].reshape(bp, sp, SB_KV_HEADS, SB_HEAD_DIM)

    ps = norm_matmul(hs, mix_g, w_in)
    q = ps[:, :hq].reshape(bs, ss, SB_KV_HEADS, SB_GROUP, SB_HEAD_DIM)
    q = q.transpose(0, 2, 3, 1, 4).reshape(bs, SB_KV_HEADS * SB_GROUP * ss, SB_HEAD_DIM)
    k_s = ps[:, EV_K:EV_K + hkv].reshape(bs, ss, SB_KV_HEADS, SB_HEAD_DIM)
    v_s = ps[:, EV_V:EV_V + hkv].reshape(bs, ss, SB_KV_HEADS, SB_HEAD_DIM)
    pw = page_size * SB_KV_HEADS
    pad = ((0, 0), (0, pw - ss * SB_KV_HEADS), (0, 0))
    k_new = jnp.pad(k_s.reshape(bs, ss * SB_KV_HEADS, SB_HEAD_DIM), pad)
    v_new = jnp.pad(v_s.reshape(bs, ss * SB_KV_HEADS, SB_HEAD_DIM), pad)
    o = sb_sample(q, k_new, v_new, k_pool.reshape(-1, SB_HEAD_DIM), v_pool.reshape(-1, SB_HEAD_DIM),
                  page_table, page_size)
    o_a = o.reshape(bs, SB_KV_HEADS, SB_GROUP, ss, SB_HEAD_DIM).transpose(0, 3, 1, 2, 4).reshape(bs * ss, hq)
    o_b, conv_s = conformer_conv(ps.reshape(bs, ss, EV_W), state_conv, cw, cb, lg, lb)
    hs = out_proj(o_a, o_b.reshape(bs * ss, CONV_CH), w1, w2, hs)
    return hp, hs, (k_p, v_p, k_s, v_s, conv_p, conv_s)


def _gate_rows(proj3, L):
    b, t, _ = proj3.shape
    nc = -(-t // L)
    g = proj3[:, :, OD_G:OD_G + 2 * ML_HEADS].transpose(0, 2, 1)
    g = jnp.pad(g, ((0, 0), (0, 0), (0, nc * L - t)))
    return g.reshape(b, 2 * ML_HEADS, nc, L).transpose(0, 2, 1, 3)


def _odd_layer(hp, hs, mix_g, w_in, w_out, b_i, b_f, ml_norm_g, q_norm_g, kv_norm_g, w_uq, w_uk, w_uv,
               ml_state, ckv_pool, kr_pool, page_table, bp, sp, bs, ss, past_len):
    hv = ML_HEADS * ML_DV
    w1, w2 = w_out[:hv], w_out[hv:]
    gate_bias = jnp.concatenate([b_i, b_f]).astype(F32)
    c_lora = w_uk.shape[0]
    w_kv = jnp.concatenate([w_uk.reshape(c_lora, -1), w_uv.reshape(c_lora, -1)], axis=1).astype(BF16)
    w_q = _w_uq(w_uq)
    chunk = LANE

    pp = norm_matmul(hp, mix_g, w_in)
    pp3 = pp.reshape(bp, sp, OD_W)
    o_c, c_p, n_p, m_p = mlstm(pp3, _gate_rows(pp3, chunk), gate_bias, ml_norm_g, None, chunk)
    qf = norm_matmul(pp, q_norm_g, w_q, col_block=OD_CQ // MLA_Q_LORA)
    cos, sin = _rope_tables(jnp.tile(jnp.arange(sp), bp))
    ckv_p, kr_p, qr = mla_prep(pp, qf, cos, sin, kv_norm_g)
    kv = norm_matmul(pp, kv_norm_g, w_kv, col_block=OD_CKV // MLA_KV_LORA)
    o_d = mla_prompt(qf, qr, kv, kr_p, bp, sp)
    hp = out_proj(o_c.reshape(bp * sp, hv), o_d, w1, w2, hp)

    ps = norm_matmul(hs, mix_g, w_in)
    ps3 = ps.reshape(bs, ss, OD_W)
    o_c, c_s, n_s, m_s = mlstm(ps3, _gate_rows(ps3, chunk), gate_bias, ml_norm_g, ml_state, chunk)
    qf = norm_matmul(ps, q_norm_g, w_q, col_block=OD_CQ // MLA_Q_LORA)
    cos, sin = _rope_tables(jnp.tile(past_len + jnp.arange(ss), bs))
    ckv_s, kr_s, qr = mla_prep(ps, qf, cos, sin, kv_norm_g)
    q_nope = qf[:, :MLA_HEADS * MLA_NOPE].reshape(bs * ss, MLA_HEADS, MLA_NOPE).transpose(1, 0, 2)
    q_lat = head_matmul(q_nope, w_uk.transpose(1, 2, 0).astype(BF16))
    q_lat = q_lat.reshape(MLA_HEADS, bs, ss, c_lora).transpose(1, 0, 2, 3).reshape(bs, MLA_HEADS * ss, c_lora)
    q_rope = qr.reshape(bs, ss, MLA_HEADS, MLA_ROPE).transpose(0, 2, 1, 3).reshape(bs, MLA_HEADS * ss, MLA_ROPE)
    page = ckv_pool.shape[1]
    ckv_new = jnp.pad(ckv_s.reshape(bs, ss, c_lora), ((0, 0), (0, page - ss), (0, 0)))
    kr_new = jnp.pad(kr_s.reshape(bs, ss, MLA_ROPE), ((0, 0), (0, page - ss), (0, 0)))
    o_lat = mla_sample(q_lat, q_rope, ckv_new, kr_new, ckv_pool, kr_pool, page_table, ss)
    o_lat = o_lat.reshape(bs, MLA_HEADS, ss, c_lora).transpose(1, 0, 2, 3).reshape(MLA_HEADS, bs * ss, c_lora)
    o_d = head_matmul(o_lat, w_uv.transpose(1, 0, 2).astype(BF16))
    o_d = o_d.transpose(1, 0, 2).reshape(bs * ss, MLA_HEADS * MLA_V)
    hs = out_proj(o_c.reshape(bs * ss, hv), o_d, w1, w2, hs)

    outs = (c_p, n_p, m_p, c_s, n_s, m_s,
            ckv_p.reshape(bp, sp, c_lora), kr_p.reshape(bp, sp, MLA_ROPE),
            ckv_s.reshape(bs, ss, c_lora), kr_s.reshape(bs, ss, MLA_ROPE))
    return hp, hs, outs


def kernel(x_prompt, x_sample, cache_sb_k, cache_sb_v, state_conv, state_mlstm_C, state_mlstm_n, state_mlstm_m, cache_mla_ckv, cache_mla_krope, page_table, ffn_norm1_g, mix_norm_g, ffn_norm2_g, ffn1_w_gate, ffn1_w_up, ffn1_w_down, ffn2_w_gate, ffn2_w_up, ffn2_w_down, even_w_in, even_w_out, conv_w, conv_b, conv_ln_g, conv_ln_b, odd_w_in, odd_w_out, mlstm_b_i, mlstm_b_f, mlstm_norm_g, mla_q_norm_g, mla_kv_norm_g, mla_w_uq, mla_w_uk, mla_w_uv, final_norm_g):
    bp, sp, d = x_prompt.shape
    bs, ss, _ = x_sample.shape
    depth = ffn_norm1_g.shape[0]
    page_size = cache_sb_k.shape[2]
    past_len = page_table.shape[1] * page_size

    hp = x_prompt.reshape(bp * sp, d)
    hs = x_sample.reshape(bs * ss, d)
    even_outs, odd_outs = [], []
    for layer in range(depth):
        w = [a[layer].astype(BF16) for a in (ffn1_w_gate, ffn1_w_up, ffn1_w_down)]
        hp = ffn(hp, ffn_norm1_g[layer], *w)
        hs = ffn(hs, ffn_norm1_g[layer], *w)
        if layer % 2 == 0:
            e = layer // 2
            hp, hs, outs = _even_layer(
                hp, hs, mix_norm_g[layer], _even_w_in(even_w_in[e]), even_w_out[e].astype(BF16),
                conv_w[e], conv_b[e], conv_ln_g[e], conv_ln_b[e], state_conv[e],
                cache_sb_k[e], cache_sb_v[e], page_table, bp, sp, bs, ss, page_size)
            even_outs.append(outs)
        else:
            o = layer // 2
            hp, hs, outs = _odd_layer(
                hp, hs, mix_norm_g[layer], _odd_w_in(odd_w_in[o]), odd_w_out[o].astype(BF16),
                mlstm_b_i[o], mlstm_b_f[o], mlstm_norm_g[o], mla_q_norm_g[o], mla_kv_norm_g[o],
                mla_w_uq[o], mla_w_uk[o], mla_w_uv[o],
                (state_mlstm_C[o], state_mlstm_n[o], state_mlstm_m[o]),
                cache_mla_ckv[o], cache_mla_krope[o], page_table, bp, sp, bs, ss, past_len)
            odd_outs.append(outs)
        w = [a[layer].astype(BF16) for a in (ffn2_w_gate, ffn2_w_up, ffn2_w_down)]
        hp = ffn(hp, ffn_norm2_g[layer], *w)
        hs = ffn(hs, ffn_norm2_g[layer], *w)

    y_prompt = final_norm(hp, final_norm_g).reshape(bp, sp, d)
    y_sample = final_norm(hs, final_norm_g).reshape(bs, ss, d)
    ev = [jnp.stack(x) for x in zip(*even_outs)]
    od = [jnp.stack(x) for x in zip(*odd_outs)]
    return (y_prompt, y_sample, *ev, *od)
```

```python
import functools

import numpy as np
import jax
import jax.numpy as jnp
from jax import lax
from jax.experimental import pallas as pl
from jax.experimental.pallas import tpu as pltpu

F32 = jnp.float32
BF16 = jnp.bfloat16

SB_HEADS = 8
SB_KV_HEADS = 2
SB_HEAD_DIM = 128
SB_GROUP = SB_HEADS // SB_KV_HEADS
CONV_CH = 1024
CONV_WIDTH = 31
ML_HEADS = 4
ML_DK = 128
ML_DV = 256
GATE_CAP = 15.0
MLA_HEADS = 8
MLA_Q_LORA = 512
MLA_KV_LORA = 512
MLA_NOPE = 128
MLA_ROPE = 64
MLA_V = 128
ROPE_BASE = 10000.0
EPS = 1e-6
LN_EPS = 1e-5

LANE = 128
SUBLANE = 8
VMEM_LIMIT = 56 * 1024 * 1024
VMEM_HEADROOM = 4 * 1024 * 1024

NEG_BIG = -1e30

EV_Q = 0
EV_VAL = EV_Q + SB_HEADS * SB_HEAD_DIM
EV_GATE = EV_VAL + CONV_CH
EV_K = EV_GATE + CONV_CH
EV_V = EV_K + SB_KV_HEADS * SB_HEAD_DIM
EV_W = EV_V + SB_KV_HEADS * SB_HEAD_DIM

OD_Q = 0
OD_K = OD_Q + ML_HEADS * ML_DK
OD_V = OD_K + ML_HEADS * ML_DK
OD_O = OD_V + ML_HEADS * ML_DV
OD_CQ = OD_O + ML_HEADS * ML_DV
OD_CKV = OD_CQ + MLA_Q_LORA
OD_KR = OD_CKV + MLA_KV_LORA
OD_G = OD_KR + 2 * MLA_ROPE
OD_W = OD_G + LANE

QF_NOPE = 0
QF_ROPE = MLA_HEADS * MLA_NOPE
QF_ROPE_SW = QF_ROPE + MLA_HEADS * MLA_ROPE
QF_W = QF_ROPE_SW + MLA_HEADS * MLA_ROPE


def _cparams(*sem):
    return pltpu.CompilerParams(dimension_semantics=sem, vmem_limit_bytes=VMEM_LIMIT)


def _pick(n, cands):
    for c in cands:
        if n % c == 0:
            return c
    return n


def _dot(a, b):
    return jnp.dot(a, b, preferred_element_type=F32)


def _dot_nt(a, b):
    return lax.dot_general(a, b, (((1,), (1,)), ((), ())), preferred_element_type=F32)


def _rms(x, g):
    return x * lax.rsqrt(jnp.mean(x * x, axis=-1, keepdims=True) + EPS) * g


def _log_sigmoid(x):
    return jnp.minimum(x, 0.0) - jnp.log1p(jnp.exp(-jnp.abs(x)))


def _ffn_kernel(x_ref, g_ref, wg_ref, wu_ref, wd_ref, o_ref, xn_ref, acc_ref):
    j = pl.program_id(1)

    @pl.when(j == 0)
    def _():
        xn_ref[...] = _rms(x_ref[...], g_ref[...]).astype(BF16)
        acc_ref[...] = jnp.zeros_like(acc_ref)

    xn = xn_ref[...]
    h = _dot(xn, wg_ref[...])
    u = _dot(xn, wu_ref[...])
    a = (h * jax.nn.sigmoid(h) * u).astype(BF16)
    acc_ref[...] += _dot(a, wd_ref[...])

    @pl.when(j == pl.num_programs(1) - 1)
    def _():
        o_ref[...] = x_ref[...] + 0.5 * acc_ref[...]


def ffn(x, g, wg, wu, wd):
    t, d = x.shape
    f = wg.shape[1]
    tm = _pick(t, (512, 256, 128, 64, 32, 16, 8))
    tf = _pick(f, (512, 256, 128))
    return pl.pallas_call(
        _ffn_kernel,
        grid=(t // tm, f // tf),
        in_specs=[
            pl.BlockSpec((tm, d), lambda i, j: (i, 0)),
            pl.BlockSpec((1, d), lambda i, j: (0, 0)),
            pl.BlockSpec((d, tf), lambda i, j: (0, j)),
            pl.BlockSpec((d, tf), lambda i, j: (0, j)),
            pl.BlockSpec((tf, d), lambda i, j: (j, 0)),
        ],
        out_specs=pl.BlockSpec((tm, d), lambda i, j: (i, 0)),
        out_shape=jax.ShapeDtypeStruct((t, d), F32),
        scratch_shapes=[pltpu.VMEM((tm, d), BF16), pltpu.VMEM((tm, d), F32)],
        compiler_params=_cparams("parallel", "arbitrary"),
        name="ffn",
    )(x, g.reshape(1, d), wg, wu, wd)


def _nmm_kernel(x_ref, g_ref, w_ref, o_ref, xn_ref):
    @pl.when(pl.program_id(1) == 0)
    def _():
        xn_ref[...] = _rms(x_ref[...], g_ref[...]).astype(BF16)

    o_ref[...] = _dot(xn_ref[...], w_ref[...])


def _nmm_tiles(t, k, n):
    budget = VMEM_LIMIT - VMEM_HEADROOM
    for tn in (n, 512, 256, 128):
        if n % tn:
            continue
        for tm in (512, 256, 128, 64, 32, 16, 8, t):
            if t % tm:
                continue
            need = 2 * tm * k * 4 + 2 * k * tn * 2 + 2 * tm * tn * 4 + tm * k * 2
            if need <= budget:
                return tm, tn
    raise ValueError(f"no norm_matmul tiling for {(t, k, n)}")


def norm_matmul(x, g, w, col_block=0):
    t = x.shape[0]
    k, n = w.shape
    tm, tn = _nmm_tiles(t, k, n)
    return pl.pallas_call(
        _nmm_kernel,
        grid=(t // tm, n // tn),
        in_specs=[
            pl.BlockSpec((tm, k), lambda i, j: (i, col_block)),
            pl.BlockSpec((1, k), lambda i, j: (0, 0)),
            pl.BlockSpec((k, tn), lambda i, j: (0, j)),
        ],
        out_specs=pl.BlockSpec((tm, tn), lambda i, j: (i, j)),
        out_shape=jax.ShapeDtypeStruct((t, n), F32),
        scratch_shapes=[pltpu.VMEM((tm, k), BF16)],
        compiler_params=_cparams("parallel", "arbitrary"),
        name="norm_matmul",
    )(x, g.reshape(1, k), w)


def _mm2_kernel(a_ref, b_ref, w1_ref, w2_ref, r_ref, o_ref):
    o_ref[...] = (r_ref[...] + _dot(a_ref[...].astype(BF16), w1_ref[...])
                  + _dot(b_ref[...].astype(BF16), w2_ref[...]))


def out_proj(a, b, w1, w2, res):
    t, ka = a.shape
    kb = b.shape[1]
    n = w1.shape[1]
    tm = _pick(t, (256, 128, 64, 32, 16, 8))
    return pl.pallas_call(
        _mm2_kernel,
        grid=(t // tm,),
        in_specs=[
            pl.BlockSpec((tm, ka), lambda i: (i, 0)),
            pl.BlockSpec((tm, kb), lambda i: (i, 0)),
            pl.BlockSpec((ka, n), lambda i: (0, 0)),
            pl.BlockSpec((kb, n), lambda i: (0, 0)),
            pl.BlockSpec((tm, n), lambda i: (i, 0)),
        ],
        out_specs=pl.BlockSpec((tm, n), lambda i: (i, 0)),
        out_shape=jax.ShapeDtypeStruct((t, n), F32),
        compiler_params=_cparams("parallel"),
        name="out_proj",
    )(a, b, w1, w2, res)


def _norm_kernel(x_ref, g_ref, o_ref):
    o_ref[...] = _rms(x_ref[...], g_ref[...])


def final_norm(x, g):
    t, d = x.shape
    tm = _pick(t, (512, 256, 128, 64, 32, 16, 8))
    return pl.pallas_call(
        _norm_kernel,
        grid=(t // tm,),
        in_specs=[pl.BlockSpec((tm, d), lambda i: (i, 0)), pl.BlockSpec((1, d), lambda i: (0, 0))],
        out_specs=pl.BlockSpec((tm, d), lambda i: (i, 0)),
        out_shape=jax.ShapeDtypeStruct((t, d), F32),
        compiler_params=_cparams("parallel"),
        name="final_norm",
    )(x, g.reshape(1, d))


def _bmm_kernel(x_ref, w_ref, o_ref):
    o_ref[...] = _dot(x_ref[...].astype(BF16), w_ref[...])


def head_matmul(x, w):
    h, m, k = x.shape
    n = w.shape[2]
    return pl.pallas_call(
        _bmm_kernel,
        grid=(h,),
        in_specs=[pl.BlockSpec((None, m, k), lambda i: (i, 0, 0)),
                  pl.BlockSpec((None, k, n), lambda i: (i, 0, 0))],
        out_specs=pl.BlockSpec((None, m, n), lambda i: (i, 0, 0)),
        out_shape=jax.ShapeDtypeStruct((h, m, n), F32),
        compiler_params=_cparams("parallel"),
        name="head_matmul",
    )(x, w)


def _tri_schedule(nq, reverse):
    qs, ks = [], []
    for qi in range(nq):
        order = range(qi, -1, -1) if reverse else range(qi + 1)
        for kj in order:
            qs.append(qi)
            ks.append(kj)
    return jnp.asarray(np.array(qs, np.int32)), jnp.asarray(np.array(ks, np.int32))


def _strict_suffix_matrix(n):
    j = np.arange(n)[:, None]
    s = np.arange(n)[None, :]
    return jnp.asarray((j > s).astype(np.float32), dtype=BF16)


def _split_bf16(x):
    hi = x.astype(BF16)
    lo = (x - hi.astype(F32)).astype(BF16)
    return hi, lo


def _sbp_kernel(qt_ref, kt_ref, q_ref, k_ref, v_ref, tri_ref, o_ref, qb_ref, acc_ref, car_ref, *, tb, scale):
    step = pl.program_id(2)
    qi = qt_ref[step]
    kj = kt_ref[step]
    tri = tri_ref[...]

    def sweep(diag):
        kb = k_ref[...].astype(BF16)
        vb = v_ref[...].astype(BF16)
        z = _dot_nt(qb_ref[...], kb) * scale
        lg = jnp.log(1.0 + jnp.exp(-jnp.abs(z)))
        l1m = -jnp.maximum(z, 0.0) - lg
        if diag:
            row = lax.broadcasted_iota(jnp.int32, z.shape, 0)
            col = lax.broadcasted_iota(jnp.int32, z.shape, 1)
            mask = col < (row % tb)
            l1m = jnp.where(mask, l1m, 0.0)
        hi, lo = _split_bf16(l1m)
        suffix = _dot(hi, tri) + _dot(lo, tri) + car_ref[...]
        w = jnp.exp(jnp.minimum(z, 0.0) - lg + suffix)
        if diag:
            w = jnp.where(mask, w, 0.0)
        acc_ref[...] += _dot(w.astype(BF16), vb)
        car_ref[...] += jnp.sum(l1m, axis=-1, keepdims=True)

    @pl.when(kj == qi)
    def _():
        for gi in range(SB_GROUP):
            qb_ref[gi * tb:(gi + 1) * tb, :] = q_ref[:, gi * SB_HEAD_DIM:(gi + 1) * SB_HEAD_DIM].astype(BF16)
        acc_ref[...] = jnp.zeros_like(acc_ref)
        car_ref[...] = jnp.zeros_like(car_ref)
        sweep(True)

    @pl.when(kj < qi)
    def _():
        sweep(False)

    @pl.when(kj == 0)
    def _():
        for gi in range(SB_GROUP):
            o_ref[:, gi * SB_HEAD_DIM:(gi + 1) * SB_HEAD_DIM] = acc_ref[gi * tb:(gi + 1) * tb, :]


def sb_prompt(proj, b, s):
    tb = _pick(s, (256, 128))
    nq = s // tb
    qt, kt = _tri_schedule(nq, reverse=True)
    gw = SB_GROUP * SB_HEAD_DIM
    kcol = EV_K // SB_HEAD_DIM
    vcol = EV_V // SB_HEAD_DIM
    grid_spec = pltpu.PrefetchScalarGridSpec(
        num_scalar_prefetch=2,
        grid=(b, SB_KV_HEADS, int(qt.shape[0])),
        in_specs=[
            pl.BlockSpec((tb, gw), lambda bi, kv, st, qt, kt: (bi * nq + qt[st], kv)),
            pl.BlockSpec((tb, SB_HEAD_DIM), lambda bi, kv, st, qt, kt: (bi * nq + kt[st], kcol + kv)),
            pl.BlockSpec((tb, SB_HEAD_DIM), lambda bi, kv, st, qt, kt: (bi * nq + kt[st], vcol + kv)),
            pl.BlockSpec((tb, tb), lambda bi, kv, st, qt, kt: (0, 0)),
        ],
        out_specs=pl.BlockSpec((tb, gw), lambda bi, kv, st, qt, kt: (bi * nq + qt[st], kv)),
        scratch_shapes=[pltpu.VMEM((SB_GROUP * tb, SB_HEAD_DIM), BF16),
                        pltpu.VMEM((SB_GROUP * tb, SB_HEAD_DIM), F32),
                        pltpu.VMEM((SB_GROUP * tb, 1), F32)],
    )
    return pl.pallas_call(
        functools.partial(_sbp_kernel, tb=tb, scale=SB_HEAD_DIM ** -0.5),
        grid_spec=grid_spec,
        out_shape=jax.ShapeDtypeStruct((b * s, SB_HEADS * SB_HEAD_DIM), F32),
        compiler_params=_cparams("parallel", "parallel", "arbitrary"),
        name="sb_prompt",
    )(qt, kt, proj, proj, proj, _strict_suffix_matrix(tb))


def _sbs_kernel(pt_ref, q_ref, kn_ref, vn_ref, tri_ref, own_ref, *rest, pg, scale):
    k_refs = rest[:pg]
    v_refs = rest[pg:2 * pg]
    o_ref, acc_ref, car_ref = rest[2 * pg:]
    g = pl.program_id(1)
    rows, width = q_ref.shape[0], kn_ref.shape[0]
    tokens = rows // (SB_KV_HEADS * SB_GROUP)
    qb = q_ref[...].astype(BF16)
    tri = tri_ref[...]

    def scores(k_list, maskf):
        npg = len(k_list)
        kb = jnp.concatenate([r[...].astype(BF16) for r in k_list], axis=0)
        z = _dot_nt(qb, kb) * scale
        zs = jnp.concatenate([z[:, p * width:(p + 1) * width] for p in range(npg)], axis=0)
        lg = jnp.log(1.0 + jnp.exp(-jnp.abs(zs)))
        l1m = (-jnp.maximum(zs, 0.0) - lg) * maskf
        hi, lo = _split_bf16(l1m)
        suffix = _dot(hi, tri) + _dot(lo, tri)
        tot = jnp.sum(l1m, axis=-1, keepdims=True)
        return jnp.minimum(zs, 0.0) - lg + suffix, tot

    def weighted(logw, tot, carry, v_list, maskf):
        npg = len(v_list)
        cars = [None] * npg
        for p in reversed(range(npg)):
            cars[p] = carry
            carry = carry + tot[p * rows:(p + 1) * rows]
        w = (jnp.exp(logw + jnp.concatenate(cars, axis=0)) * maskf).astype(BF16)
        w = jnp.concatenate([w[p * rows:(p + 1) * rows] for p in range(npg)], axis=1)
        vb = jnp.concatenate([r[...].astype(BF16) for r in v_list], axis=0)
        return _dot(w, vb), carry

    def sweep(groups):
        parts = [scores(k_list, maskf) for k_list, _, maskf in groups]
        carry = car_ref[...]
        out = acc_ref[...]
        for (logw, tot), (_, v_list, maskf) in zip(parts, groups):
            o, carry = weighted(logw, tot, carry, v_list, maskf)
            out = out + o
        acc_ref[...] = out
        car_ref[...] = carry

    @pl.when(g == 0)
    def _():
        acc_ref[...] = jnp.zeros_like(acc_ref)
        car_ref[...] = jnp.zeros_like(car_ref)
        row = lax.broadcasted_iota(jnp.int32, (rows, width), 0)
        col = lax.broadcasted_iota(jnp.int32, (rows, width), 1)
        before = (col // SB_KV_HEADS) < (row % tokens)
        sweep([([kn_ref], [vn_ref], jnp.where(before, own_ref[0:rows, :], 0.0))])

    half = (pg + 1) // 2
    own_hi = own_ref[0:(pg - half) * rows, :]
    own_lo = own_ref[0:half * rows, :]
    groups = [(k_refs[half:], v_refs[half:], own_hi), (k_refs[:half], v_refs[:half], own_lo)]
    sweep([grp for grp in groups if len(grp[0])])

    @pl.when(g == pl.num_programs(1) - 1)
    def _():
        o_ref[...] = acc_ref[...]


def sb_sample(q, k_new, v_new, k_pool, v_pool, page_table, page_size):
    n, rows, dh = q.shape
    n_pages = page_table.shape[1]
    pw = page_size * SB_KV_HEADS
    pg = _pick(n_pages, (16, 8, 4, 2, 1))
    ng = n_pages // pg
    row_kv = (np.arange(pg * rows) % rows) // (rows // SB_KV_HEADS)
    own = (np.arange(pw)[None, :] % SB_KV_HEADS == row_kv[:, None]).astype(np.float32)

    def page_map(i):
        return lambda ni, g, pt: (pt[ni * n_pages + (ng - 1 - g) * pg + i], 0)

    pool_specs = [pl.BlockSpec((pw, dh), page_map(i)) for i in range(pg)]
    grid_spec = pltpu.PrefetchScalarGridSpec(
        num_scalar_prefetch=1,
        grid=(n, ng),
        in_specs=[
            pl.BlockSpec((None, rows, dh), lambda ni, g, pt: (ni, 0, 0)),
            pl.BlockSpec((None, pw, dh), lambda ni, g, pt: (ni, 0, 0)),
            pl.BlockSpec((None, pw, dh), lambda ni, g, pt: (ni, 0, 0)),
            pl.BlockSpec((pw, pw), lambda ni, g, pt: (0, 0)),
            pl.BlockSpec((pg * rows, pw), lambda ni, g, pt: (0, 0)),
        ] + pool_specs + pool_specs,
        out_specs=pl.BlockSpec((None, rows, dh), lambda ni, g, pt: (ni, 0, 0)),
        scratch_shapes=[pltpu.VMEM((rows, dh), F32), pltpu.VMEM((rows, 1), F32)],
    )
    return pl.pallas_call(
        functools.partial(_sbs_kernel, pg=pg, scale=dh ** -0.5),
        grid_spec=grid_spec,
        out_shape=jax.ShapeDtypeStruct((n, rows, dh), F32),
        compiler_params=_cparams("parallel", "arbitrary"),
        name="sb_sample",
    )(page_table.reshape(-1), q, k_new, v_new, _strict_suffix_matrix(pw), jnp.asarray(own),
      *([k_pool] * pg), *([v_pool] * pg))


def _conv_kernel(*refs, tt, rb, has_past):
    if has_past:
        val_ref, gate_ref, past_ref, cw_ref, cb_ref, lg_ref, lb_ref, o_ref, st_ref, full_ref, sh_ref = refs
    else:
        val_ref, gate_ref, cw_ref, cb_ref, lg_ref, lb_ref, o_ref, st_ref, full_ref, sh_ref = refs
    i = pl.program_id(1)
    hist = CONV_WIDTH - 1
    base = 32 - hist

    @pl.when(i == 0)
    def _():
        full_ref[0:32, :] = jnp.zeros((32, CONV_CH), F32)
        if has_past:
            full_ref[base:32, :] = past_ref[...]

    full_ref[32:32 + tt, :] = val_ref[...] * jax.nn.sigmoid(gate_ref[...])

    for r0 in range(0, tt, rb):
        ys = []
        for c0 in range(0, CONV_CH, LANE):
            acc = jnp.zeros((rb, LANE), F32)
            for phase in range(SUBLANE):
                taps = range(phase, CONV_WIDTH, SUBLANE)
                span = taps[-1] - phase
                start = r0 + base + phase
                slot = sh_ref.at[phase % 2]
                slot[0:rb + span, :] = full_ref[start:start + rb + span, c0:c0 + LANE]
                for w in taps:
                    acc = acc + slot[w - phase:w - phase + rb, :] * cw_ref[w:w + 1, c0:c0 + LANE]
            ys.append(acc)
        y = jnp.concatenate(ys, axis=-1) + cb_ref[...]
        mu = jnp.mean(y, axis=-1, keepdims=True)
        var = jnp.mean(jnp.square(y - mu), axis=-1, keepdims=True)
        yn = (y - mu) * lax.rsqrt(var + LN_EPS) * lg_ref[...] + lb_ref[...]
        o_ref[r0:r0 + rb, :] = yn * jax.nn.sigmoid(yn)

    @pl.when(i == pl.num_programs(1) - 1)
    def _():
        st_ref[...] = full_ref[tt + base:tt + 32, :]

    @pl.when(i < pl.num_programs(1) - 1)
    def _():
        full_ref[0:32, :] = full_ref[tt:tt + 32, :]


def conformer_conv(proj3, past, cw, cb, lg, lb):
    b, t, _ = proj3.shape
    tt = _pick(t, (128, 64, 32, 16, 8))
    rb = _pick(tt, (64, 32, 16, 8))
    c = CONV_CH
    hist = CONV_WIDTH - 1
    vcol, gcol = EV_VAL // c, EV_GATE // c
    in_specs = [pl.BlockSpec((None, tt, c), lambda bi, i: (bi, i, vcol)),
                pl.BlockSpec((None, tt, c), lambda bi, i: (bi, i, gcol))]
    args = [proj3, proj3]
    if past is not None:
        in_specs.append(pl.BlockSpec((None, hist, c), lambda bi, i: (bi, 0, 0)))
        args.append(past)
    in_specs += [pl.BlockSpec((CONV_WIDTH, c), lambda bi, i: (0, 0))] + [pl.BlockSpec((1, c), lambda bi, i: (0, 0))] * 3
    args += [cw, cb.reshape(1, c), lg.reshape(1, c), lb.reshape(1, c)]
    return pl.pallas_call(
        functools.partial(_conv_kernel, tt=tt, rb=rb, has_past=past is not None),
        grid=(b, t // tt),
        in_specs=in_specs,
        out_specs=[pl.BlockSpec((None, tt, c), lambda bi, i: (bi, i, 0)),
                   pl.BlockSpec((None, hist, c), lambda bi, i: (bi, 0, 0))],
        out_shape=[jax.ShapeDtypeStruct((b, t, c), F32), jax.ShapeDtypeStruct((b, hist, c), F32)],
        scratch_shapes=[pltpu.VMEM((32 + tt, c), F32), pltpu.VMEM((2, rb + 32, LANE), F32)],
        compiler_params=_cparams("parallel", "arbitrary"),
        name="conformer_conv",
    )(*args)


def _mlstm_kernel(*refs, tb, L, zero_init):
    if zero_init:
        (q_ref, k_ref, v_ref, og_ref, gr_ref, gb_ref, ng_ref,
         h_ref, co_ref, no_ref, mo_ref, c_s, n_s, m_s, qp, kp, vp) = refs
    else:
        (q_ref, k_ref, v_ref, og_ref, gr_ref, gb_ref, ng_ref, c0_ref, n0_ref, m0_ref,
         h_ref, co_ref, no_ref, mo_ref, c_s, n_s, m_s, qp, kp, vp) = refs
    ci = pl.program_id(1)

    @pl.when(ci == 0)
    def _():
        if zero_init:
            c_s[...] = jnp.zeros_like(c_s)
            n_s[...] = jnp.zeros_like(n_s)
            m_s[...] = jnp.zeros_like(m_s)
        else:
            c_s[...] = c0_ref[...]
            n_s[...] = n0_ref[...]
            m_s[...] = m0_ref[...]

    if tb < L:
        qp[...] = jnp.zeros_like(qp)
        kp[...] = jnp.zeros_like(kp)
        vp[...] = jnp.zeros_like(vp)
        qp[0:tb, :] = q_ref[...]
        kp[0:tb, :] = k_ref[...]
        vp[0:tb, :] = v_ref[...]
        q_src, k_src, v_src = qp, kp, vp
    else:
        q_src, k_src, v_src = q_ref, k_ref, v_ref

    pre = gr_ref[...] + gb_ref[...]
    capped = GATE_CAP * jnp.tanh(pre / GATE_CAP)
    lane = lax.broadcasted_iota(jnp.int32, (2 * ML_HEADS, L), 1)
    gate_row = lax.broadcasted_iota(jnp.int32, (2 * ML_HEADS, L), 0)
    logg = jnp.where(gate_row < ML_HEADS, capped, _log_sigmoid(capped))
    if tb < L:
        logg = jnp.where(lane < tb, logg, jnp.where(gate_row < ML_HEADS, NEG_BIG, 0.0))

    rt = lax.broadcasted_iota(jnp.int32, (L, L), 0)
    cs = lax.broadcasted_iota(jnp.int32, (L, L), 1)
    eye = rt == cs
    causal = cs <= rt

    def to_col(r):
        return jnp.sum(jnp.where(eye, r, 0.0), axis=1, keepdims=True)

    for h in range(ML_HEADS):
        li_r = logg[h:h + 1, :]
        lf_r = logg[ML_HEADS + h:ML_HEADS + h + 1, :]
        lf_c = to_col(lf_r)
        b_c = jnp.sum(jnp.where(causal, lf_r, 0.0), axis=1, keepdims=True)
        b_r = jnp.sum(jnp.where(rt <= cs, lf_c, 0.0), axis=0, keepdims=True)
        d = jnp.where(causal, b_c - b_r + li_r, -jnp.inf)
        m_prev = m_s[h][:, 0:1]
        inter = b_c + m_prev
        m_c = jnp.maximum(inter, jnp.max(d, axis=1, keepdims=True))
        q = q_src[:, h * ML_DK:(h + 1) * ML_DK] * (ML_DK ** -0.5)
        k = k_src[:, h * ML_DK:(h + 1) * ML_DK]
        v = v_src[:, h * ML_DV:(h + 1) * ML_DV]
        qb, kb, vb = q.astype(BF16), k.astype(BF16), v.astype(BF16)
        sm = _dot_nt(qb, kb) * jnp.exp(d - m_c)
        dec = jnp.exp(inter - m_c)
        c_prev = c_s[h]
        n_prev = n_s[h]
        num = dec * _dot(qb, c_prev.astype(BF16)) + _dot(sm.astype(BF16), vb)
        den = dec * jnp.sum(q * n_prev, axis=1, keepdims=True) + jnp.sum(sm, axis=1, keepdims=True)
        hh = num / jnp.maximum(jnp.abs(den), jnp.exp(-m_c))
        hn = hh * lax.rsqrt(jnp.mean(hh * hh, axis=-1, keepdims=True) + EPS)
        hn = hn * ng_ref[:, h * ML_DV:(h + 1) * ML_DV]
        og = og_ref[:, h * ML_DV:(h + 1) * ML_DV]
        h_ref[:, h * ML_DV:(h + 1) * ML_DV] = hn[0:tb, :] * jax.nn.sigmoid(og)

        m_new = m_c[L - 1:L, :]
        dec_state = jnp.exp(inter[L - 1:L, :] - m_new)
        w_end_r = jnp.exp(b_r[:, L - 1:L] - b_r + li_r - m_new)
        w_end_c = to_col(w_end_r)
        kt = k.T.astype(BF16)
        c_s[h] = dec_state * c_prev + _dot(kt, (w_end_c * v).astype(BF16))
        n_s[h] = dec_state * n_prev + jnp.sum(w_end_c * k, axis=0, keepdims=True)
        m_s[h] = jnp.broadcast_to(m_new, (1, LANE))

    @pl.when(ci == pl.num_programs(1) - 1)
    def _():
        co_ref[...] = c_s[...]
        no_ref[...] = n_s[...]
        mo_ref[...] = m_s[...]


def mlstm(proj3, gates_row, gate_bias, norm_g, state, L):
    b, t, _ = proj3.shape
    nc = gates_row.shape[1]
    tb = t // nc
    hk, hv = ML_HEADS * ML_DK, ML_HEADS * ML_DV
    zero_init = state is None
    in_specs = [
        pl.BlockSpec((None, tb, hk), lambda bi, ci: (bi, ci, OD_Q // hk)),
        pl.BlockSpec((None, tb, hk), lambda bi, ci: (bi, ci, OD_K // hk)),
        pl.BlockSpec((None, tb, hv), lambda bi, ci: (bi, ci, OD_V // hv)),
        pl.BlockSpec((None, tb, hv), lambda bi, ci: (bi, ci, OD_O // hv)),
        pl.BlockSpec((None, None, 2 * ML_HEADS, L), lambda bi, ci: (bi, ci, 0, 0)),
        pl.BlockSpec((2 * ML_HEADS, 1), lambda bi, ci: (0, 0)),
        pl.BlockSpec((1, hv), lambda bi, ci: (0, 0)),
    ]
    args = [proj3, proj3, proj3, proj3, gates_row, gate_bias.reshape(2 * ML_HEADS, 1), norm_g.reshape(1, hv)]
    state_specs = [
        pl.BlockSpec((None, ML_HEADS, ML_DK, ML_DV), lambda bi, ci: (bi, 0, 0, 0)),
        pl.BlockSpec((None, ML_HEADS, 1, ML_DK), lambda bi, ci: (bi, 0, 0, 0)),
        pl.BlockSpec((None, ML_HEADS, 1, LANE), lambda bi, ci: (bi, 0, 0, 0)),
    ]
    if not zero_init:
        c0, n0, m0 = state
        in_specs += state_specs
        args += [c0, n0.reshape(b, ML_HEADS, 1, ML_DK),
                 jnp.broadcast_to(m0.reshape(b, ML_HEADS, 1, 1), (b, ML_HEADS, 1, LANE))]
    h, c, n, m = pl.pallas_call(
        functools.partial(_mlstm_kernel, tb=tb, L=L, zero_init=zero_init),
        grid=(b, nc),
        in_specs=in_specs,
        out_specs=[pl.BlockSpec((None, tb, hv), lambda bi, ci: (bi, ci, 0))] + state_specs,
        out_shape=[jax.ShapeDtypeStruct((b, t, hv), F32),
                   jax.ShapeDtypeStruct((b, ML_HEADS, ML_DK, ML_DV), F32),
                   jax.ShapeDtypeStruct((b, ML_HEADS, 1, ML_DK), F32),
                   jax.ShapeDtypeStruct((b, ML_HEADS, 1, LANE), F32)],
        scratch_shapes=[pltpu.VMEM((ML_HEADS, ML_DK, ML_DV), F32),
                        pltpu.VMEM((ML_HEADS, 1, ML_DK), F32),
                        pltpu.VMEM((ML_HEADS, 1, LANE), F32),
                        pltpu.VMEM((L, hk), F32), pltpu.VMEM((L, hk), F32), pltpu.VMEM((L, hv), F32)],
        compiler_params=_cparams("parallel", "arbitrary"),
        name="mlstm",
    )(*args)
    return h, c, n.reshape(b, ML_HEADS, ML_DK), m[:, :, 0, 0]


def _mla_prep_kernel(ckv_ref, kr_ref, qa_ref, qb_ref, cos_ref, sin_ref, g_ref, ckv_o, kr_o, qr_o):
    ckv_o[...] = _rms(ckv_ref[...], g_ref[...])
    cos = cos_ref[...]
    sin = sin_ref[...]
    lane = lax.broadcasted_iota(jnp.int32, cos.shape, 1)
    y = kr_ref[...] * jnp.where(lane < MLA_ROPE, cos, sin)
    kr_o[...] = y[:, :MLA_ROPE] + y[:, MLA_ROPE:]
    for j in range(qa_ref.shape[1] // LANE):
        sl = slice(j * LANE, (j + 1) * LANE)
        qr_o[:, sl] = qa_ref[:, sl] * cos + qb_ref[:, sl] * sin


def mla_prep(proj, qf, cos, sin, kv_norm_g):
    t = proj.shape[0]
    tm = _pick(t, (512, 256, 128, 64, 32, 16, 8))
    rw = MLA_HEADS * MLA_ROPE
    return pl.pallas_call(
        _mla_prep_kernel,
        grid=(t // tm,),
        in_specs=[
            pl.BlockSpec((tm, MLA_KV_LORA), lambda i: (i, OD_CKV // MLA_KV_LORA)),
            pl.BlockSpec((tm, LANE), lambda i: (i, OD_KR // LANE)),
            pl.BlockSpec((tm, rw), lambda i: (i, QF_ROPE // rw)),
            pl.BlockSpec((tm, rw), lambda i: (i, QF_ROPE_SW // rw)),
            pl.BlockSpec((tm, LANE), lambda i: (i, 0)),
            pl.BlockSpec((tm, LANE), lambda i: (i, 0)),
            pl.BlockSpec((1, MLA_KV_LORA), lambda i: (0, 0)),
        ],
        out_specs=[pl.BlockSpec((tm, MLA_KV_LORA), lambda i: (i, 0)),
                   pl.BlockSpec((tm, MLA_ROPE), lambda i: (i, 0)),
                   pl.BlockSpec((tm, rw), lambda i: (i, 0))],
        out_shape=[jax.ShapeDtypeStruct((t, MLA_KV_LORA), F32),
                   jax.ShapeDtypeStruct((t, MLA_ROPE), F32),
                   jax.ShapeDtypeStruct((t, rw), F32)],
        compiler_params=_cparams("parallel"),
        name="mla_prep",
    )(proj, proj, qf, qf, cos, sin, kv_norm_g.reshape(1, MLA_KV_LORA))


def _mlap_kernel(qt_ref, kt_ref, qn_ref, qr_ref, kn_ref, kr_ref, v_ref, o_ref,
                 qc, kc, acc_ref, m_ref, l_ref, *, tb, scale):
    step = pl.program_id(1)
    qi = qt_ref[step]
    kj = kt_ref[step]
    hw = qc.shape[1] // MLA_HEADS

    @pl.when(kj == 0)
    def _():
        qc[...] = jnp.zeros_like(qc)
        kc[...] = jnp.zeros_like(kc)
        for h in range(MLA_HEADS):
            qc[:, h * hw:h * hw + MLA_NOPE] = qn_ref[:, h * MLA_NOPE:(h + 1) * MLA_NOPE].astype(BF16)
            qc[:, h * hw + MLA_NOPE:h * hw + MLA_NOPE + MLA_ROPE] = (
                qr_ref[:, h * MLA_ROPE:(h + 1) * MLA_ROPE].astype(BF16))
        acc_ref[...] = jnp.zeros_like(acc_ref)
        m_ref[...] = jnp.full_like(m_ref, NEG_BIG)
        l_ref[...] = jnp.zeros_like(l_ref)

    def sweep(diag):
        krb = kr_ref[...].astype(BF16)
        for h in range(MLA_HEADS):
            kc[:, h * hw:h * hw + MLA_NOPE] = kn_ref[:, h * MLA_NOPE:(h + 1) * MLA_NOPE].astype(BF16)
            kc[:, h * hw + MLA_NOPE:h * hw + MLA_NOPE + MLA_ROPE] = krb
        vb = v_ref[...].astype(BF16)
        if diag:
            row = lax.broadcasted_iota(jnp.int32, (tb, tb), 0)
            col = lax.broadcasted_iota(jnp.int32, (tb, tb), 1)
            mask = col <= row
        for h in range(MLA_HEADS):
            vs = slice(h * MLA_V, (h + 1) * MLA_V)
            s = _dot_nt(qc[:, h * hw:(h + 1) * hw], kc[:, h * hw:(h + 1) * hw]) * scale
            if diag:
                s = jnp.where(mask, s, NEG_BIG)
            m_prev = m_ref[h]
            m_new = jnp.maximum(m_prev, jnp.max(s, axis=-1, keepdims=True))
            alpha = jnp.exp(m_prev - m_new)
            p = jnp.exp(s - jnp.concatenate([m_new] * (tb // LANE), axis=1))
            if diag:
                p = jnp.where(mask, p, 0.0)
            l_new = alpha * l_ref[h]
            for c in range(tb // LANE):
                l_new = l_new + p[:, c * LANE:(c + 1) * LANE]
            l_ref[h] = l_new
            acc_ref[:, vs] = alpha * acc_ref[:, vs] + _dot(p.astype(BF16), vb[:, vs])
            m_ref[h] = m_new

    @pl.when(kj < qi)
    def _():
        sweep(False)

    @pl.when(kj == qi)
    def _():
        sweep(True)
        for h in range(MLA_HEADS):
            vs = slice(h * MLA_V, (h + 1) * MLA_V)
            o_ref[:, vs] = acc_ref[:, vs] / jnp.sum(l_ref[h], axis=-1, keepdims=True)


def mla_prompt(qf, qr, kv, kr, b, s):
    assert MLA_V == LANE and MLA_NOPE + MLA_ROPE <= 2 * MLA_NOPE
    tb = _pick(s, (256, 128))
    nq = s // tb
    qt, kt = _tri_schedule(nq, reverse=False)
    nw, rw, vw = MLA_HEADS * MLA_NOPE, MLA_HEADS * MLA_ROPE, MLA_HEADS * MLA_V
    grid_spec = pltpu.PrefetchScalarGridSpec(
        num_scalar_prefetch=2,
        grid=(b, int(qt.shape[0])),
        in_specs=[
            pl.BlockSpec((tb, nw), lambda bi, st, qt, kt: (bi * nq + qt[st], QF_NOPE // nw)),
            pl.BlockSpec((tb, rw), lambda bi, st, qt, kt: (bi * nq + qt[st], 0)),
            pl.BlockSpec((tb, nw), lambda bi, st, qt, kt: (bi * nq + kt[st], 0)),
            pl.BlockSpec((tb, MLA_ROPE), lambda bi, st, qt, kt: (bi * nq + kt[st], 0)),
            pl.BlockSpec((tb, vw), lambda bi, st, qt, kt: (bi * nq + kt[st], nw // vw)),
        ],
        out_specs=pl.BlockSpec((tb, vw), lambda bi, st, qt, kt: (bi * nq + qt[st], 0)),
        scratch_shapes=[pltpu.VMEM((tb, 2 * nw), BF16), pltpu.VMEM((tb, 2 * nw), BF16),
                        pltpu.VMEM((tb, vw), F32),
                        pltpu.VMEM((MLA_HEADS, tb, LANE), F32), pltpu.VMEM((MLA_HEADS, tb, LANE), F32)],
    )
    return pl.pallas_call(
        functools.partial(_mlap_kernel, tb=tb, scale=(MLA_NOPE + MLA_ROPE) ** -0.5),
        grid_spec=grid_spec,
        out_shape=jax.ShapeDtypeStruct((b * s, vw), F32),
        compiler_params=_cparams("parallel", "arbitrary"),
        name="mla_prompt",
    )(qt, kt, qf, qr, kv, kr, kv)


def _mlas_kernel(pt_ref, ql_ref, qr_ref, cn_ref, rn_ref, *rest, pg, scale, tokens):
    c_refs = rest[:pg]
    r_refs = rest[pg:2 * pg]
    o_ref, acc_ref, m_ref, l_ref = rest[2 * pg:]
    g = pl.program_id(1)
    rows, width = ql_ref.shape[0], cn_ref.shape[0]
    qlb = ql_ref[...].astype(BF16)
    qrb = qr_ref[...].astype(BF16)

    def partial_softmax(c_list, r_list, mask):
        cb = jnp.concatenate([c[...].astype(BF16) for c in c_list], axis=0)
        rb = jnp.concatenate([r[...].astype(BF16) for r in r_list], axis=1)
        s = (_dot_nt(qlb, cb) + _dot(qrb, rb)) * scale
        if mask is not None:
            s = jnp.where(mask, s, NEG_BIG)
        m = jnp.max(s, axis=-1, keepdims=True)
        p = jnp.exp(s - m)
        if mask is not None:
            p = jnp.where(mask, p, 0.0)
        return m, jnp.sum(p, axis=-1, keepdims=True), _dot(p.astype(BF16), cb)

    def sweep(groups):
        parts = [partial_softmax(*grp) for grp in groups]
        m_prev = m_ref[...]
        m_new = m_prev
        for m, _, _ in parts:
            m_new = jnp.maximum(m_new, m)
        alpha = jnp.exp(m_prev - m_new)
        l_new = alpha * l_ref[...]
        acc = alpha * acc_ref[...]
        for m, l_part, acc_part in parts:
            a = jnp.exp(m - m_new)
            l_new = l_new + a * l_part
            acc = acc + a * acc_part
        l_ref[...] = l_new
        acc_ref[...] = acc
        m_ref[...] = m_new

    @pl.when(g == 0)
    def _():
        acc_ref[...] = jnp.zeros_like(acc_ref)
        m_ref[...] = jnp.full_like(m_ref, NEG_BIG)
        l_ref[...] = jnp.zeros_like(l_ref)
        row = lax.broadcasted_iota(jnp.int32, (rows, width), 0)
        col = lax.broadcasted_iota(jnp.int32, (rows, width), 1)
        sweep([([cn_ref], [rn_ref], col <= (row % tokens))])

    half = (pg + 1) // 2
    groups = [(c_refs[:half], r_refs[:half], None), (c_refs[half:], r_refs[half:], None)]
    sweep([grp for grp in groups if len(grp[0])])

    @pl.when(g == pl.num_programs(1) - 1)
    def _():
        o_ref[...] = acc_ref[...] / l_ref[...]


def mla_sample(q_lat, q_rope, ckv_new, kr_new, ckv_pool, kr_pool, page_table, tokens):
    n, rows, c = q_lat.shape
    r = q_rope.shape[2]
    page = ckv_pool.shape[1]
    n_pages = page_table.shape[1]
    pg = _pick(n_pages, (16, 8, 4, 2, 1))
    ng = n_pages // pg

    def page_map(i):
        return lambda ni, g, pt: (pt[ni * n_pages + g * pg + i], 0, 0)

    grid_spec = pltpu.PrefetchScalarGridSpec(
        num_scalar_prefetch=1,
        grid=(n, ng),
        in_specs=[
            pl.BlockSpec((None, rows, c), lambda ni, g, pt: (ni, 0, 0)),
            pl.BlockSpec((None, rows, r), lambda ni, g, pt: (ni, 0, 0)),
            pl.BlockSpec((None, page, c), lambda ni, g, pt: (ni, 0, 0)),
            pl.BlockSpec((None, r, page), lambda ni, g, pt: (ni, 0, 0)),
        ] + [pl.BlockSpec((None, page, c), page_map(i)) for i in range(pg)]
          + [pl.BlockSpec((None, r, page), page_map(i)) for i in range(pg)],
        out_specs=pl.BlockSpec((None, rows, c), lambda ni, g, pt: (ni, 0, 0)),
        scratch_shapes=[pltpu.VMEM((rows, c), F32), pltpu.VMEM((rows, 1), F32), pltpu.VMEM((rows, 1), F32)],
    )
    return pl.pallas_call(
        functools.partial(_mlas_kernel, pg=pg, scale=(MLA_NOPE + MLA_ROPE) ** -0.5, tokens=tokens),
        grid_spec=grid_spec,
        out_shape=jax.ShapeDtypeStruct((n, rows, c), F32),
        compiler_params=_cparams("parallel", "arbitrary"),
        name="mla_sample",
    )(page_table.reshape(-1), q_lat, q_rope, ckv_new, kr_new, *([ckv_pool] * pg), *([kr_pool] * pg))


def _swap_half(w):
    half = w.shape[-1] // 2
    return jnp.concatenate([-w[..., half:], w[..., :half]], axis=-1)


def _even_w_in(w):
    sizes = (SB_HEADS * SB_HEAD_DIM, SB_KV_HEADS * SB_HEAD_DIM, SB_KV_HEADS * SB_HEAD_DIM, 2 * CONV_CH)
    q, k, v, glu = jnp.split(w, np.cumsum(sizes)[:-1].tolist(), axis=1)
    return jnp.concatenate([q, glu, k, v], axis=1).astype(BF16)


def _odd_w_in(w):
    d = w.shape[0]
    sizes = (ML_HEADS * ML_DK, ML_HEADS * ML_DK, ML_HEADS * ML_DV, ML_HEADS, ML_HEADS, ML_HEADS * ML_DV,
             MLA_Q_LORA, MLA_KV_LORA, MLA_ROPE)
    mq, mk, mv, mi, mf, mo, cq, ckv, kr = jnp.split(w, np.cumsum(sizes)[:-1].tolist(), axis=1)
    pad = jnp.zeros((d, LANE - 2 * ML_HEADS), w.dtype)
    return jnp.concatenate([mq, mk, mv, mo, cq, ckv, kr, _swap_half(kr), mi, mf, pad], axis=1).astype(BF16)


def _w_uq(w):
    k = w.shape[0]
    w3 = w.reshape(k, MLA_HEADS, MLA_NOPE + MLA_ROPE)
    nope = w3[..., :MLA_NOPE].reshape(k, -1)
    rope = w3[..., MLA_NOPE:]
    return jnp.concatenate([nope, rope.reshape(k, -1), _swap_half(rope).reshape(k, -1)], axis=1).astype(BF16)


def _rope_tables(pos):
    half = MLA_ROPE // 2
    inv = ROPE_BASE ** (-jnp.arange(half, dtype=F32) / half)
    ang = pos.astype(F32)[:, None] * inv[None, :]
    reps = LANE // half
    return jnp.tile(jnp.cos(ang), (1, reps)), jnp.tile(jnp.sin(ang), (1, reps))


def _even_layer(hp, hs, mix_g, w_in, w_out, cw, cb, lg, lb, state_conv, k_pool, v_pool, page_table,
                bp, sp, bs, ss, page_size):
    hq = SB_HEADS * SB_HEAD_DIM
    hkv = SB_KV_HEADS * SB_HEAD_DIM
    w1, w2 = w_out[:hq], w_out[hq:]

    pp = norm_matmul(hp, mix_g, w_in)
    o_a = sb_prompt(pp, bp, sp)
    o_b, conv_p = conformer_conv(pp.reshape(bp, sp, EV_W), None, cw, cb, lg, lb)
    hp = out_proj(o_a, o_b.reshape(bp * sp, CONV_CH), w1, w2, hp)
    k_p = pp[:, EV_K:EV_K + hkv].reshape(bp, sp, SB_KV_HEADS, SB_HEAD_DIM)
    v_p = pp[:, EV_V:EV_V + hkv].reshape(bp, sp, SB_KV_HEADS, SB_HEAD_DIM)

    ps = norm_matmul(hs, mix_g, w_in)
    q = ps[:, :hq].reshape(bs, ss, SB_KV_HEADS, SB_GROUP, SB_HEAD_DIM)
    q = q.transpose(0, 2, 3, 1, 4).reshape(bs, SB_KV_HEADS * SB_GROUP * ss, SB_HEAD_DIM)
    k_s = ps[:, EV_K:EV_K + hkv].reshape(bs, ss, SB_KV_HEADS, SB_HEAD_DIM)
    v_s = ps[:, EV_V:EV_V + hkv].reshape(bs, ss, SB_KV_HEADS, SB_HEAD_DIM)
    pw = page_size * SB_KV_HEADS
    pad = ((0, 0), (0, pw - ss * SB_KV_HEADS), (0, 0))
    k_new = jnp.pad(k_s.reshape(bs, ss * SB_KV_HEADS, SB_HEAD_DIM), pad)
    v_new = jnp.pad(v_s.reshape(bs, ss * SB_KV_HEADS, SB_HEAD_DIM), pad)
    o = sb_sample(q, k_new, v_new, k_pool.reshape(-1, SB_HEAD_DIM), v_pool.reshape(-1, SB_HEAD_DIM),
                  page_table, page_size)
    o_a = o.reshape(bs, SB_KV_HEADS, SB_GROUP, ss, SB_HEAD_DIM).transpose(0, 3, 1, 2, 4).reshape(bs * ss, hq)
    o_b, conv_s = conformer_conv(ps.reshape(bs, ss, EV_W), state_conv, cw, cb, lg, lb)
    hs = out_proj(o_a, o_b.reshape(bs * ss, CONV_CH), w1, w2, hs)
    return hp, hs, (k_p, v_p, k_s, v_s, conv_p, conv_s)


def _gate_rows(proj3, L):
    b, t, _ = proj3.shape
    nc = -(-t // L)
    g = proj3[:, :, OD_G:OD_G + 2 * ML_HEADS].transpose(0, 2, 1)
    g = jnp.pad(g, ((0, 0), (0, 0), (0, nc * L - t)))
    return g.reshape(b, 2 * ML_HEADS, nc, L).transpose(0, 2, 1, 3)


def _odd_layer(hp, hs, mix_g, w_in, w_out, b_i, b_f, ml_norm_g, q_norm_g, kv_norm_g, w_uq, w_uk, w_uv,
               ml_state, ckv_pool, kr_pool, page_table, bp, sp, bs, ss, past_len):
    hv = ML_HEADS * ML_DV
    w1, w2 = w_out[:hv], w_out[hv:]
    gate_bias = jnp.concatenate([b_i, b_f]).astype(F32)
    c_lora = w_uk.shape[0]
    w_kv = jnp.concatenate([w_uk.reshape(c_lora, -1), w_uv.reshape(c_lora, -1)], axis=1).astype(BF16)
    w_q = _w_uq(w_uq)
    chunk = LANE

    pp = norm_matmul(hp, mix_g, w_in)
    pp3 = pp.reshape(bp, sp, OD_W)
    o_c, c_p, n_p, m_p = mlstm(pp3, _gate_rows(pp3, chunk), gate_bias, ml_norm_g, None, chunk)
    qf = norm_matmul(pp, q_norm_g, w_q, col_block=OD_CQ // MLA_Q_LORA)
    cos, sin = _rope_tables(jnp.tile(jnp.arange(sp), bp))
    ckv_p, kr_p, qr = mla_prep(pp, qf, cos, sin, kv_norm_g)
    kv = norm_matmul(pp, kv_norm_g, w_kv, col_block=OD_CKV // MLA_KV_LORA)
    o_d = mla_prompt(qf, qr, kv, kr_p, bp, sp)
    hp = out_proj(o_c.reshape(bp * sp, hv), o_d, w1, w2, hp)

    ps = norm_matmul(hs, mix_g, w_in)
    ps3 = ps.reshape(bs, ss, OD_W)
    o_c, c_s, n_s, m_s = mlstm(ps3, _gate_rows(ps3, chunk), gate_bias, ml_norm_g, ml_state, chunk)
    qf = norm_matmul(ps, q_norm_g, w_q, col_block=OD_CQ // MLA_Q_LORA)
    cos, sin = _rope_tables(jnp.tile(past_len + jnp.arange(ss), bs))
    ckv_s, kr_s, qr = mla_prep(ps, qf, cos, sin, kv_norm_g)
    q_nope = qf[:, :MLA_HEADS * MLA_NOPE].reshape(bs * ss, MLA_HEADS, MLA_NOPE).transpose(1, 0, 2)
    q_lat = head_matmul(q_nope, w_uk.transpose(1, 2, 0).astype(BF16))
    q_lat = q_lat.reshape(MLA_HEADS, bs, ss, c_lora).transpose(1, 0, 2, 3).reshape(bs, MLA_HEADS * ss, c_lora)
    q_rope = qr.reshape(bs, ss, MLA_HEADS, MLA_ROPE).transpose(0, 2, 1, 3).reshape(bs, MLA_HEADS * ss, MLA_ROPE)
    page = ckv_pool.shape[1]
    ckv_new = jnp.pad(ckv_s.reshape(bs, ss, c_lora), ((0, 0), (0, page - ss), (0, 0)))
    kr_new = jnp.pad(kr_s.reshape(bs, ss, MLA_ROPE).transpose(0, 2, 1), ((0, 0), (0, 0), (0, page - ss)))
    o_lat = mla_sample(q_lat, q_rope, ckv_new, kr_new, ckv_pool, kr_pool.transpose(0, 2, 1), page_table, ss)
    o_lat = o_lat.reshape(bs, MLA_HEADS, ss, c_lora).transpose(1, 0, 2, 3).reshape(MLA_HEADS, bs * ss, c_lora)
    o_d = head_matmul(o_lat, w_uv.transpose(1, 0, 2).astype(BF16))
    o_d = o_d.transpose(1, 0, 2).reshape(bs * ss, MLA_HEADS * MLA_V)
    hs = out_proj(o_c.reshape(bs * ss, hv), o_d, w1, w2, hs)

    outs = (c_p, n_p, m_p, c_s, n_s, m_s,
            ckv_p.reshape(bp, sp, c_lora), kr_p.reshape(bp, sp, MLA_ROPE),
            ckv_s.reshape(bs, ss, c_lora), kr_s.reshape(bs, ss, MLA_ROPE))
    return hp, hs, outs


def kernel(x_prompt, x_sample, cache_sb_k, cache_sb_v, state_conv, state_mlstm_C, state_mlstm_n, state_mlstm_m, cache_mla_ckv, cache_mla_krope, page_table, ffn_norm1_g, mix_norm_g, ffn_norm2_g, ffn1_w_gate, ffn1_w_up, ffn1_w_down, ffn2_w_gate, ffn2_w_up, ffn2_w_down, even_w_in, even_w_out, conv_w, conv_b, conv_ln_g, conv_ln_b, odd_w_in, odd_w_out, mlstm_b_i, mlstm_b_f, mlstm_norm_g, mla_q_norm_g, mla_kv_norm_g, mla_w_uq, mla_w_uk, mla_w_uv, final_norm_g):
    bp, sp, d = x_prompt.shape
    bs, ss, _ = x_sample.shape
    depth = ffn_norm1_g.shape[0]
    page_size = cache_sb_k.shape[2]
    past_len = page_table.shape[1] * page_size

    hp = x_prompt.reshape(bp * sp, d)
    hs = x_sample.reshape(bs * ss, d)
    even_outs, odd_outs = [], []
    for layer in range(depth):
        w = [a[layer].astype(BF16) for a in (ffn1_w_gate, ffn1_w_up, ffn1_w_down)]
        hp = ffn(hp, ffn_norm1_g[layer], *w)
        hs = ffn(hs, ffn_norm1_g[layer], *w)
        if layer % 2 == 0:
            e = layer // 2
            hp, hs, outs = _even_layer(
                hp, hs, mix_norm_g[layer], _even_w_in(even_w_in[e]), even_w_out[e].astype(BF16),
                conv_w[e], conv_b[e], conv_ln_g[e], conv_ln_b[e], state_conv[e],
                cache_sb_k[e], cache_sb_v[e], page_table, bp, sp, bs, ss, page_size)
            even_outs.append(outs)
        else:
            o = layer // 2
            hp, hs, outs = _odd_layer(
                hp, hs, mix_norm_g[layer], _odd_w_in(odd_w_in[o]), odd_w_out[o].astype(BF16),
                mlstm_b_i[o], mlstm_b_f[o], mlstm_norm_g[o], mla_q_norm_g[o], mla_kv_norm_g[o],
                mla_w_uq[o], mla_w_uk[o], mla_w_uv[o],
                (state_mlstm_C[o], state_mlstm_n[o], state_mlstm_m[o]),
                cache_mla_ckv[o], cache_mla_krope[o], page_table, bp, sp, bs, ss, past_len)
            odd_outs.append(outs)
        w = [a[layer].astype(BF16) for a in (ffn2_w_gate, ffn2_w_up, ffn2_w_down)]
        hp = ffn(hp, ffn_norm2_g[layer], *w)
        hs = ffn(hs, ffn_norm2_g[layer], *w)

    y_prompt = final_norm(hp, final_norm_g).reshape(bp, sp, d)
    y_sample = final_norm(hs, final_norm_g).reshape(bs, ss, d)
    ev = [jnp.stack(x) for x in zip(*even_outs)]
    od = [jnp.stack(x) for x in zip(*odd_outs)]
    return (y_prompt, y_sample, *ev, *od)
```

```python
import functools

import numpy as np
import jax
import jax.numpy as jnp
from jax import lax
from jax.experimental import pallas as pl
from jax.experimental.pallas import tpu as pltpu

F32 = jnp.float32
BF16 = jnp.bfloat16

SB_HEADS = 8
SB_KV_HEADS = 2
SB_HEAD_DIM = 128
SB_GROUP = SB_HEADS // SB_KV_HEADS
CONV_CH = 1024
CONV_WIDTH = 31
ML_HEADS = 4
ML_DK = 128
ML_DV = 256
GATE_CAP = 15.0
MLA_HEADS = 8
MLA_Q_LORA = 512
MLA_KV_LORA = 512
MLA_NOPE = 128
MLA_ROPE = 64
MLA_V = 128
ROPE_BASE = 10000.0
EPS = 1e-6
LN_EPS = 1e-5

LANE = 128
SUBLANE = 8
VMEM_LIMIT = 56 * 1024 * 1024
VMEM_HEADROOM = 4 * 1024 * 1024

NEG_BIG = -1e30
LOG2E = 1.4426950408889634

EV_Q = 0
EV_VAL = EV_Q + SB_HEADS * SB_HEAD_DIM
EV_GATE = EV_VAL + CONV_CH
EV_K = EV_GATE + CONV_CH
EV_V = EV_K + SB_KV_HEADS * SB_HEAD_DIM
EV_W = EV_V + SB_KV_HEADS * SB_HEAD_DIM

OD_Q = 0
OD_K = OD_Q + ML_HEADS * ML_DK
OD_V = OD_K + ML_HEADS * ML_DK
OD_O = OD_V + ML_HEADS * ML_DV
OD_CQ = OD_O + ML_HEADS * ML_DV
OD_CKV = OD_CQ + MLA_Q_LORA
OD_KR = OD_CKV + MLA_KV_LORA
OD_G = OD_KR + 2 * MLA_ROPE
OD_W = OD_G + LANE

QF_NOPE = 0
QF_ROPE = MLA_HEADS * MLA_NOPE
QF_ROPE_SW = QF_ROPE + MLA_HEADS * MLA_ROPE
QF_W = QF_ROPE_SW + MLA_HEADS * MLA_ROPE


def _cparams(*sem):
    return pltpu.CompilerParams(dimension_semantics=sem, vmem_limit_bytes=VMEM_LIMIT)


def _pick(n, cands):
    for c in cands:
        if n % c == 0:
            return c
    return n


def _dot(a, b):
    return jnp.dot(a, b, preferred_element_type=F32)


def _dot_nt(a, b):
    return lax.dot_general(a, b, (((1,), (1,)), ((), ())), preferred_element_type=F32)


def _rms(x, g):
    return x * lax.rsqrt(jnp.mean(x * x, axis=-1, keepdims=True) + EPS) * g


def _log_sigmoid(x):
    return jnp.minimum(x, 0.0) - jnp.log1p(jnp.exp(-jnp.abs(x)))


def _ffn_kernel(*refs, final):
    if final:
        x_ref, g_ref, wg_ref, wu_ref, wd_ref, fg_ref, o_ref, xn_ref = refs
    else:
        x_ref, g_ref, wg_ref, wu_ref, wd_ref, o_ref, xn_ref = refs
    j = pl.program_id(1)

    @pl.when(j == 0)
    def _():
        x = x_ref[...]
        xn_ref[...] = _rms(x, g_ref[...]).astype(BF16)
        o_ref[...] = x

    xn = xn_ref[...]
    h = _dot(xn, wg_ref[...])
    u = _dot(xn, wu_ref[...])
    a = (h * jax.nn.sigmoid(h) * u * 0.5).astype(BF16)
    o_ref[...] += _dot(a, wd_ref[...])

    if final:
        @pl.when(j == pl.num_programs(1) - 1)
        def _():
            o_ref[...] = _rms(o_ref[...], fg_ref[...])


def _ffn_tiles(t, d, f):
    tf = _pick(f, (512, 256, 128))
    budget = VMEM_LIMIT - VMEM_HEADROOM
    for tm in (1024, 512, 256, 128, 64, 32, 16, 8, t):
        if t % tm:
            continue
        need = 4 * tm * d * 4 + tm * d * 2 + 3 * 2 * d * tf * 2 + tm * tf * 4
        if need <= budget:
            return tm, tf
    raise ValueError(f"no ffn tiling for {(t, d, f)}")


def ffn(x, g, wg, wu, wd, final_g=None):
    t, d = x.shape
    f = wg.shape[1]
    tm, tf = _ffn_tiles(t, d, f)
    final = final_g is not None
    in_specs = [
        pl.BlockSpec((tm, d), lambda i, j: (i, 0)),
        pl.BlockSpec((1, d), lambda i, j: (0, 0)),
        pl.BlockSpec((d, tf), lambda i, j: (0, j)),
        pl.BlockSpec((d, tf), lambda i, j: (0, j)),
        pl.BlockSpec((tf, d), lambda i, j: (j, 0)),
    ]
    args = [x, g.reshape(1, d), wg, wu, wd]
    if final:
        in_specs.append(pl.BlockSpec((1, d), lambda i, j: (0, 0)))
        args.append(final_g.reshape(1, d))
    return pl.pallas_call(
        functools.partial(_ffn_kernel, final=final),
        grid=(t // tm, f // tf),
        in_specs=in_specs,
        out_specs=pl.BlockSpec((tm, d), lambda i, j: (i, 0)),
        out_shape=jax.ShapeDtypeStruct((t, d), F32),
        scratch_shapes=[pltpu.VMEM((tm, d), BF16)],
        compiler_params=_cparams("parallel", "arbitrary"),
        name="ffn",
    )(*args)


def _nmm_kernel(x_ref, g_ref, w_ref, o_ref, xn_ref):
    @pl.when(pl.program_id(1) == 0)
    def _():
        xn_ref[...] = _rms(x_ref[...], g_ref[...]).astype(BF16)

    o_ref[...] = _dot(xn_ref[...], w_ref[...])


def _nmm_tiles(t, k, n):
    budget = VMEM_LIMIT - VMEM_HEADROOM
    for tn in (n, 512, 256, 128):
        if n % tn:
            continue
        for tm in (512, 256, 128, 64, 32, 16, 8, t):
            if t % tm:
                continue
            need = 2 * tm * k * 4 + 2 * k * tn * 2 + 2 * tm * tn * 4 + tm * k * 2
            if need <= budget:
                return tm, tn
    raise ValueError(f"no norm_matmul tiling for {(t, k, n)}")


def norm_matmul(x, g, w, col_block=0):
    t = x.shape[0]
    k, n = w.shape
    tm, tn = _nmm_tiles(t, k, n)
    return pl.pallas_call(
        _nmm_kernel,
        grid=(t // tm, n // tn),
        in_specs=[
            pl.BlockSpec((tm, k), lambda i, j: (i, col_block)),
            pl.BlockSpec((1, k), lambda i, j: (0, 0)),
            pl.BlockSpec((k, tn), lambda i, j: (0, j)),
        ],
        out_specs=pl.BlockSpec((tm, tn), lambda i, j: (i, j)),
        out_shape=jax.ShapeDtypeStruct((t, n), F32),
        scratch_shapes=[pltpu.VMEM((tm, k), BF16)],
        compiler_params=_cparams("parallel", "arbitrary"),
        name="norm_matmul",
    )(x, g.reshape(1, k), w)


def _mm2_kernel(a_ref, b_ref, w1_ref, w2_ref, r_ref, o_ref):
    o_ref[...] = (r_ref[...] + _dot(a_ref[...].astype(BF16), w1_ref[...])
                  + _dot(b_ref[...].astype(BF16), w2_ref[...]))


def out_proj(a, b, w1, w2, res):
    t, ka = a.shape
    kb = b.shape[1]
    n = w1.shape[1]
    tm = _pick(t, (256, 128, 64, 32, 16, 8))
    return pl.pallas_call(
        _mm2_kernel,
        grid=(t // tm,),
        in_specs=[
            pl.BlockSpec((tm, ka), lambda i: (i, 0)),
            pl.BlockSpec((tm, kb), lambda i: (i, 0)),
            pl.BlockSpec((ka, n), lambda i: (0, 0)),
            pl.BlockSpec((kb, n), lambda i: (0, 0)),
            pl.BlockSpec((tm, n), lambda i: (i, 0)),
        ],
        out_specs=pl.BlockSpec((tm, n), lambda i: (i, 0)),
        out_shape=jax.ShapeDtypeStruct((t, n), F32),
        compiler_params=_cparams("parallel"),
        name="out_proj",
    )(a, b, w1, w2, res)


def _bmm_kernel(x_ref, w_ref, o_ref):
    o_ref[...] = _dot(x_ref[...].astype(BF16), w_ref[...])


def head_matmul(x, w):
    h, m, k = x.shape
    n = w.shape[2]
    return pl.pallas_call(
        _bmm_kernel,
        grid=(h,),
        in_specs=[pl.BlockSpec((None, m, k), lambda i: (i, 0, 0)),
                  pl.BlockSpec((None, k, n), lambda i: (i, 0, 0))],
        out_specs=pl.BlockSpec((None, m, n), lambda i: (i, 0, 0)),
        out_shape=jax.ShapeDtypeStruct((h, m, n), F32),
        compiler_params=_cparams("parallel"),
        name="head_matmul",
    )(x, w)


def _tri_schedule(nq, reverse):
    qs, ks = [], []
    for qi in range(nq):
        order = range(qi, -1, -1) if reverse else range(qi + 1)
        for kj in order:
            qs.append(qi)
            ks.append(kj)
    return jnp.asarray(np.array(qs, np.int32)), jnp.asarray(np.array(ks, np.int32))


def _strict_suffix_matrix(n):
    j = np.arange(n)[:, None]
    s = np.arange(n)[None, :]
    return jnp.asarray((j > s).astype(np.float32), dtype=BF16)


def _split_bf16(x):
    hi = x.astype(BF16)
    lo = (x - hi.astype(F32)).astype(BF16)
    return hi, lo


def _sbp_kernel(qt_ref, kt_ref, q_ref, k_ref, v_ref, tri_ref, o_ref, qb_ref, acc_ref, car_ref, *, tb, scale):
    step = pl.program_id(2)
    qi = qt_ref[step]
    kj = kt_ref[step]
    tri = tri_ref[...]

    def sweep(diag):
        kb = k_ref[...].astype(BF16)
        vb = v_ref[...].astype(BF16)
        z = _dot_nt(qb_ref[...], kb) * (scale * LOG2E)
        lg = jnp.log2(1.0 + jnp.exp2(-jnp.abs(z)))
        l1m = -jnp.maximum(z, 0.0) - lg
        if diag:
            row = lax.broadcasted_iota(jnp.int32, z.shape, 0)
            col = lax.broadcasted_iota(jnp.int32, z.shape, 1)
            mask = col < (row % tb)
            l1m = jnp.where(mask, l1m, 0.0)
        hi, lo = _split_bf16(l1m)
        suffix = _dot(hi, tri) + _dot(lo, tri) + car_ref[...]
        w = jnp.exp2(jnp.minimum(z, 0.0) - lg + suffix)
        if diag:
            w = jnp.where(mask, w, 0.0)
        acc_ref[...] += _dot(w.astype(BF16), vb)
        car_ref[...] += jnp.sum(l1m, axis=-1, keepdims=True)

    @pl.when(kj == qi)
    def _():
        for gi in range(SB_GROUP):
            qb_ref[gi * tb:(gi + 1) * tb, :] = q_ref[:, gi * SB_HEAD_DIM:(gi + 1) * SB_HEAD_DIM].astype(BF16)
        acc_ref[...] = jnp.zeros_like(acc_ref)
        car_ref[...] = jnp.zeros_like(car_ref)
        sweep(True)

    @pl.when(kj < qi)
    def _():
        sweep(False)

    @pl.when(kj == 0)
    def _():
        for gi in range(SB_GROUP):
            o_ref[:, gi * SB_HEAD_DIM:(gi + 1) * SB_HEAD_DIM] = acc_ref[gi * tb:(gi + 1) * tb, :]


def sb_prompt(proj, b, s):
    tb = _pick(s, (256, 128))
    nq = s // tb
    qt, kt = _tri_schedule(nq, reverse=True)
    gw = SB_GROUP * SB_HEAD_DIM
    kcol = EV_K // SB_HEAD_DIM
    vcol = EV_V // SB_HEAD_DIM
    grid_spec = pltpu.PrefetchScalarGridSpec(
        num_scalar_prefetch=2,
        grid=(b, SB_KV_HEADS, int(qt.shape[0])),
        in_specs=[
            pl.BlockSpec((tb, gw), lambda bi, kv, st, qt, kt: (bi * nq + qt[st], kv)),
            pl.BlockSpec((tb, SB_HEAD_DIM), lambda bi, kv, st, qt, kt: (bi * nq + kt[st], kcol + kv)),
            pl.BlockSpec((tb, SB_HEAD_DIM), lambda bi, kv, st, qt, kt: (bi * nq + kt[st], vcol + kv)),
            pl.BlockSpec((tb, tb), lambda bi, kv, st, qt, kt: (0, 0)),
        ],
        out_specs=pl.BlockSpec((tb, gw), lambda bi, kv, st, qt, kt: (bi * nq + qt[st], kv)),
        scratch_shapes=[pltpu.VMEM((SB_GROUP * tb, SB_HEAD_DIM), BF16),
                        pltpu.VMEM((SB_GROUP * tb, SB_HEAD_DIM), F32),
                        pltpu.VMEM((SB_GROUP * tb, 1), F32)],
    )
    return pl.pallas_call(
        functools.partial(_sbp_kernel, tb=tb, scale=SB_HEAD_DIM ** -0.5),
        grid_spec=grid_spec,
        out_shape=jax.ShapeDtypeStruct((b * s, SB_HEADS * SB_HEAD_DIM), F32),
        compiler_params=_cparams("parallel", "parallel", "arbitrary"),
        name="sb_prompt",
    )(qt, kt, proj, proj, proj, _strict_suffix_matrix(tb))


def _sbs_kernel(pt_ref, q_ref, kn_ref, vn_ref, tri_ref, own_ref, *rest, pg, scale):
    k_refs = rest[:pg]
    v_refs = rest[pg:2 * pg]
    o_ref, acc_ref, car_ref = rest[2 * pg:]
    g = pl.program_id(1)
    rows, width = q_ref.shape[0], kn_ref.shape[0]
    tokens = rows // (SB_KV_HEADS * SB_GROUP)
    qb = q_ref[...].astype(BF16)
    tri = tri_ref[...]

    def scores(k_list, maskf):
        npg = len(k_list)
        kb = jnp.concatenate([r[...].astype(BF16) for r in k_list], axis=0)
        z = _dot_nt(qb, kb) * (scale * LOG2E)
        zs = jnp.concatenate([z[:, p * width:(p + 1) * width] for p in range(npg)], axis=0)
        lg = jnp.log2(1.0 + jnp.exp2(-jnp.abs(zs)))
        l1m = (-jnp.maximum(zs, 0.0) - lg) * maskf
        hi, lo = _split_bf16(l1m)
        suffix = _dot(hi, tri) + _dot(lo, tri)
        tot = jnp.sum(l1m, axis=-1, keepdims=True)
        return jnp.minimum(zs, 0.0) - lg + suffix, tot

    def weighted(logw, tot, carry, v_list, maskf):
        npg = len(v_list)
        cars = [None] * npg
        for p in reversed(range(npg)):
            cars[p] = carry
            carry = carry + tot[p * rows:(p + 1) * rows]
        w = (jnp.exp2(logw + jnp.concatenate(cars, axis=0)) * maskf).astype(BF16)
        w = jnp.concatenate([w[p * rows:(p + 1) * rows] for p in range(npg)], axis=1)
        vb = jnp.concatenate([r[...].astype(BF16) for r in v_list], axis=0)
        return _dot(w, vb), carry

    def sweep(groups):
        parts = [scores(k_list, maskf) for k_list, _, maskf in groups]
        carry = car_ref[...]
        out = acc_ref[...]
        for (logw, tot), (_, v_list, maskf) in zip(parts, groups):
            o, carry = weighted(logw, tot, carry, v_list, maskf)
            out = out + o
        acc_ref[...] = out
        car_ref[...] = carry

    @pl.when(g == 0)
    def _():
        acc_ref[...] = jnp.zeros_like(acc_ref)
        car_ref[...] = jnp.zeros_like(car_ref)
        row = lax.broadcasted_iota(jnp.int32, (rows, width), 0)
        col = lax.broadcasted_iota(jnp.int32, (rows, width), 1)
        before = (col // SB_KV_HEADS) < (row % tokens)
        sweep([([kn_ref], [vn_ref], jnp.where(before, own_ref[0:rows, :], 0.0))])

    half = (pg + 1) // 2
    own_hi = own_ref[0:(pg - half) * rows, :]
    own_lo = own_ref[0:half * rows, :]
    groups = [(k_refs[half:], v_refs[half:], own_hi), (k_refs[:half], v_refs[:half], own_lo)]
    sweep([grp for grp in groups if len(grp[0])])

    @pl.when(g == pl.num_programs(1) - 1)
    def _():
        o_ref[...] = acc_ref[...]


def sb_sample(q, k_new, v_new, k_pool, v_pool, page_table, page_size):
    n, rows, dh = q.shape
    n_pages = page_table.shape[1]
    pw = page_size * SB_KV_HEADS
    pg = _pick(n_pages, (16, 8, 4, 2, 1))
    ng = n_pages // pg
    row_kv = (np.arange(pg * rows) % rows) // (rows // SB_KV_HEADS)
    own = (np.arange(pw)[None, :] % SB_KV_HEADS == row_kv[:, None]).astype(np.float32)

    def page_map(i):
        return lambda ni, g, pt: (pt[ni * n_pages + (ng - 1 - g) * pg + i], 0)

    pool_specs = [pl.BlockSpec((pw, dh), page_map(i)) for i in range(pg)]
    grid_spec = pltpu.PrefetchScalarGridSpec(
        num_scalar_prefetch=1,
        grid=(n, ng),
        in_specs=[
            pl.BlockSpec((None, rows, dh), lambda ni, g, pt: (ni, 0, 0)),
            pl.BlockSpec((None, pw, dh), lambda ni, g, pt: (ni, 0, 0)),
            pl.BlockSpec((None, pw, dh), lambda ni, g, pt: (ni, 0, 0)),
            pl.BlockSpec((pw, pw), lambda ni, g, pt: (0, 0)),
            pl.BlockSpec((pg * rows, pw), lambda ni, g, pt: (0, 0)),
        ] + pool_specs + pool_specs,
        out_specs=pl.BlockSpec((None, rows, dh), lambda ni, g, pt: (ni, 0, 0)),
        scratch_shapes=[pltpu.VMEM((rows, dh), F32), pltpu.VMEM((rows, 1), F32)],
    )
    return pl.pallas_call(
        functools.partial(_sbs_kernel, pg=pg, scale=dh ** -0.5),
        grid_spec=grid_spec,
        out_shape=jax.ShapeDtypeStruct((n, rows, dh), F32),
        compiler_params=_cparams("parallel", "arbitrary"),
        name="sb_sample",
    )(page_table.reshape(-1), q, k_new, v_new, _strict_suffix_matrix(pw), jnp.asarray(own),
      *([k_pool] * pg), *([v_pool] * pg))


def _conv_kernel(*refs, tt, rb, has_past):
    if has_past:
        val_ref, gate_ref, past_ref, cw_ref, cb_ref, lg_ref, lb_ref, o_ref, st_ref, full_ref, sh_ref = refs
    else:
        val_ref, gate_ref, cw_ref, cb_ref, lg_ref, lb_ref, o_ref, st_ref, full_ref, sh_ref = refs
    i = pl.program_id(1)
    hist = CONV_WIDTH - 1
    base = 32 - hist

    @pl.when(i == 0)
    def _():
        full_ref[0:32, :] = jnp.zeros((32, CONV_CH), F32)
        if has_past:
            full_ref[base:32, :] = past_ref[...]

    full_ref[32:32 + tt, :] = val_ref[...] * jax.nn.sigmoid(gate_ref[...])

    for r0 in range(0, tt, rb):
        ys = []
        for c0 in range(0, CONV_CH, LANE):
            acc = jnp.zeros((rb, LANE), F32)
            for phase in range(SUBLANE):
                taps = range(phase, CONV_WIDTH, SUBLANE)
                span = taps[-1] - phase
                start = r0 + base + phase
                slot = sh_ref.at[phase % 2]
                slot[0:rb + span, :] = full_ref[start:start + rb + span, c0:c0 + LANE]
                for w in taps:
                    acc = acc + slot[w - phase:w - phase + rb, :] * cw_ref[w:w + 1, c0:c0 + LANE]
            ys.append(acc)
        y = jnp.concatenate(ys, axis=-1) + cb_ref[...]
        mu = jnp.mean(y, axis=-1, keepdims=True)
        var = jnp.mean(jnp.square(y - mu), axis=-1, keepdims=True)
        yn = (y - mu) * lax.rsqrt(var + LN_EPS) * lg_ref[...] + lb_ref[...]
        o_ref[r0:r0 + rb, :] = yn * jax.nn.sigmoid(yn)

    @pl.when(i == pl.num_programs(1) - 1)
    def _():
        st_ref[...] = full_ref[tt + base:tt + 32, :]

    @pl.when(i < pl.num_programs(1) - 1)
    def _():
        full_ref[0:32, :] = full_ref[tt:tt + 32, :]


def conformer_conv(proj3, past, cw, cb, lg, lb):
    b, t, _ = proj3.shape
    tt = _pick(t, (128, 64, 32, 16, 8))
    rb = _pick(tt, (64, 32, 16, 8))
    c = CONV_CH
    hist = CONV_WIDTH - 1
    vcol, gcol = EV_VAL // c, EV_GATE // c
    in_specs = [pl.BlockSpec((None, tt, c), lambda bi, i: (bi, i, vcol)),
                pl.BlockSpec((None, tt, c), lambda bi, i: (bi, i, gcol))]
    args = [proj3, proj3]
    if past is not None:
        in_specs.append(pl.BlockSpec((None, hist, c), lambda bi, i: (bi, 0, 0)))
        args.append(past)
    in_specs += [pl.BlockSpec((CONV_WIDTH, c), lambda bi, i: (0, 0))] + [pl.BlockSpec((1, c), lambda bi, i: (0, 0))] * 3
    args += [cw, cb.reshape(1, c), lg.reshape(1, c), lb.reshape(1, c)]
    return pl.pallas_call(
        functools.partial(_conv_kernel, tt=tt, rb=rb, has_past=past is not None),
        grid=(b, t // tt),
        in_specs=in_specs,
        out_specs=[pl.BlockSpec((None, tt, c), lambda bi, i: (bi, i, 0)),
                   pl.BlockSpec((None, hist, c), lambda bi, i: (bi, 0, 0))],
        out_shape=[jax.ShapeDtypeStruct((b, t, c), F32), jax.ShapeDtypeStruct((b, hist, c), F32)],
        scratch_shapes=[pltpu.VMEM((32 + tt, c), F32), pltpu.VMEM((2, rb + 32, LANE), F32)],
        compiler_params=_cparams("parallel", "arbitrary"),
        name="conformer_conv",
    )(*args)


def _mlstm_kernel(*refs, tb, L, zero_init):
    if zero_init:
        (q_ref, k_ref, v_ref, og_ref, gr_ref, gb_ref, ng_ref,
         h_ref, co_ref, no_ref, mo_ref, c_s, n_s, m_s, qp, kp, vp) = refs
    else:
        (q_ref, k_ref, v_ref, og_ref, gr_ref, gb_ref, ng_ref, c0_ref, n0_ref, m0_ref,
         h_ref, co_ref, no_ref, mo_ref, c_s, n_s, m_s, qp, kp, vp) = refs
    ci = pl.program_id(1)

    @pl.when(ci == 0)
    def _():
        if zero_init:
            c_s[...] = jnp.zeros_like(c_s)
            n_s[...] = jnp.zeros_like(n_s)
            m_s[...] = jnp.zeros_like(m_s)
        else:
            c_s[...] = c0_ref[...]
            n_s[...] = n0_ref[...]
            m_s[...] = m0_ref[...]

    if tb < L:
        qp[...] = jnp.zeros_like(qp)
        kp[...] = jnp.zeros_like(kp)
        vp[...] = jnp.zeros_like(vp)
        qp[0:tb, :] = q_ref[...]
        kp[0:tb, :] = k_ref[...]
        vp[0:tb, :] = v_ref[...]
        q_src, k_src, v_src = qp, kp, vp
    else:
        q_src, k_src, v_src = q_ref, k_ref, v_ref

    pre = gr_ref[...] + gb_ref[...]
    capped = GATE_CAP * jnp.tanh(pre / GATE_CAP)
    lane = lax.broadcasted_iota(jnp.int32, (2 * ML_HEADS, L), 1)
    gate_row = lax.broadcasted_iota(jnp.int32, (2 * ML_HEADS, L), 0)
    logg = jnp.where(gate_row < ML_HEADS, capped, _log_sigmoid(capped))
    if tb < L:
        logg = jnp.where(lane < tb, logg, jnp.where(gate_row < ML_HEADS, NEG_BIG, 0.0))

    rt = lax.broadcasted_iota(jnp.int32, (L, L), 0)
    cs = lax.broadcasted_iota(jnp.int32, (L, L), 1)
    eye = rt == cs
    causal = cs <= rt

    def to_col(r):
        return jnp.sum(jnp.where(eye, r, 0.0), axis=1, keepdims=True)

    for h in range(ML_HEADS):
        li_r = logg[h:h + 1, :]
        lf_r = logg[ML_HEADS + h:ML_HEADS + h + 1, :]
        lf_c = to_col(lf_r)
        b_c = jnp.sum(jnp.where(causal, lf_r, 0.0), axis=1, keepdims=True)
        b_r = jnp.sum(jnp.where(rt <= cs, lf_c, 0.0), axis=0, keepdims=True)
        d = jnp.where(causal, b_c - b_r + li_r, -jnp.inf)
        m_prev = m_s[h][:, 0:1]
        inter = b_c + m_prev
        m_c = jnp.maximum(inter, jnp.max(d, axis=1, keepdims=True))
        q = q_src[:, h * ML_DK:(h + 1) * ML_DK] * (ML_DK ** -0.5)
        k = k_src[:, h * ML_DK:(h + 1) * ML_DK]
        v = v_src[:, h * ML_DV:(h + 1) * ML_DV]
        qb, kb, vb = q.astype(BF16), k.astype(BF16), v.astype(BF16)
        sm = _dot_nt(qb, kb) * jnp.exp(d - m_c)
        dec = jnp.exp(inter - m_c)
        c_prev = c_s[h]
        n_prev = n_s[h]
        num = dec * _dot(qb, c_prev.astype(BF16)) + _dot(sm.astype(BF16), vb)
        den = dec * jnp.sum(q * n_prev, axis=1, keepdims=True) + jnp.sum(sm, axis=1, keepdims=True)
        hh = num / jnp.maximum(jnp.abs(den), jnp.exp(-m_c))
        hn = hh * lax.rsqrt(jnp.mean(hh * hh, axis=-1, keepdims=True) + EPS)
        hn = hn * ng_ref[:, h * ML_DV:(h + 1) * ML_DV]
        og = og_ref[:, h * ML_DV:(h + 1) * ML_DV]
        h_ref[:, h * ML_DV:(h + 1) * ML_DV] = hn[0:tb, :] * jax.nn.sigmoid(og)

        m_new = m_c[L - 1:L, :]
        dec_state = jnp.exp(inter[L - 1:L, :] - m_new)
        w_end_r = jnp.exp(b_r[:, L - 1:L] - b_r + li_r - m_new)
        w_end_c = to_col(w_end_r)
        kt = k.T.astype(BF16)
        c_s[h] = dec_state * c_prev + _dot(kt, (w_end_c * v).astype(BF16))
        n_s[h] = dec_state * n_prev + jnp.sum(w_end_c * k, axis=0, keepdims=True)
        m_s[h] = jnp.broadcast_to(m_new, (1, LANE))

    @pl.when(ci == pl.num_programs(1) - 1)
    def _():
        co_ref[...] = c_s[...]
        no_ref[...] = n_s[...]
        mo_ref[...] = m_s[...]


def mlstm(proj3, gates_row, gate_bias, norm_g, state, L):
    b, t, _ = proj3.shape
    nc = gates_row.shape[1]
    tb = t // nc
    hk, hv = ML_HEADS * ML_DK, ML_HEADS * ML_DV
    zero_init = state is None
    in_specs = [
        pl.BlockSpec((None, tb, hk), lambda bi, ci: (bi, ci, OD_Q // hk)),
        pl.BlockSpec((None, tb, hk), lambda bi, ci: (bi, ci, OD_K // hk)),
        pl.BlockSpec((None, tb, hv), lambda bi, ci: (bi, ci, OD_V // hv)),
        pl.BlockSpec((None, tb, hv), lambda bi, ci: (bi, ci, OD_O // hv)),
        pl.BlockSpec((None, None, 2 * ML_HEADS, L), lambda bi, ci: (bi, ci, 0, 0)),
        pl.BlockSpec((2 * ML_HEADS, 1), lambda bi, ci: (0, 0)),
        pl.BlockSpec((1, hv), lambda bi, ci: (0, 0)),
    ]
    args = [proj3, proj3, proj3, proj3, gates_row, gate_bias.reshape(2 * ML_HEADS, 1), norm_g.reshape(1, hv)]
    state_specs = [
        pl.BlockSpec((None, ML_HEADS, ML_DK, ML_DV), lambda bi, ci: (bi, 0, 0, 0)),
        pl.BlockSpec((None, ML_HEADS, 1, ML_DK), lambda bi, ci: (bi, 0, 0, 0)),
        pl.BlockSpec((None, ML_HEADS, 1, LANE), lambda bi, ci: (bi, 0, 0, 0)),
    ]
    if not zero_init:
        c0, n0, m0 = state
        in_specs += state_specs
        args += [c0, n0.reshape(b, ML_HEADS, 1, ML_DK),
                 jnp.broadcast_to(m0.reshape(b, ML_HEADS, 1, 1), (b, ML_HEADS, 1, LANE))]
    h, c, n, m = pl.pallas_call(
        functools.partial(_mlstm_kernel, tb=tb, L=L, zero_init=zero_init),
        grid=(b, nc),
        in_specs=in_specs,
        out_specs=[pl.BlockSpec((None, tb, hv), lambda bi, ci: (bi, ci, 0))] + state_specs,
        out_shape=[jax.ShapeDtypeStruct((b, t, hv), F32),
                   jax.ShapeDtypeStruct((b, ML_HEADS, ML_DK, ML_DV), F32),
                   jax.ShapeDtypeStruct((b, ML_HEADS, 1, ML_DK), F32),
                   jax.ShapeDtypeStruct((b, ML_HEADS, 1, LANE), F32)],
        scratch_shapes=[pltpu.VMEM((ML_HEADS, ML_DK, ML_DV), F32),
                        pltpu.VMEM((ML_HEADS, 1, ML_DK), F32),
                        pltpu.VMEM((ML_HEADS, 1, LANE), F32),
                        pltpu.VMEM((L, hk), F32), pltpu.VMEM((L, hk), F32), pltpu.VMEM((L, hv), F32)],
        compiler_params=_cparams("parallel", "arbitrary"),
        name="mlstm",
    )(*args)
    return h, c, n.reshape(b, ML_HEADS, ML_DK), m[:, :, 0, 0]


def _mla_prep_kernel(ckv_ref, kr_ref, qa_ref, qb_ref, cos_ref, sin_ref, g_ref, ckv_o, kr_o, qr_o):
    ckv_o[...] = _rms(ckv_ref[...], g_ref[...])
    cos = cos_ref[...]
    sin = sin_ref[...]
    lane = lax.broadcasted_iota(jnp.int32, cos.shape, 1)
    y = kr_ref[...] * jnp.where(lane < MLA_ROPE, cos, sin)
    kr_o[...] = y[:, :MLA_ROPE] + y[:, MLA_ROPE:]
    for j in range(qa_ref.shape[1] // LANE):
        sl = slice(j * LANE, (j + 1) * LANE)
        qr_o[:, sl] = qa_ref[:, sl] * cos + qb_ref[:, sl] * sin


def mla_prep(proj, qf, cos, sin, kv_norm_g):
    t = proj.shape[0]
    tm = _pick(t, (512, 256, 128, 64, 32, 16, 8))
    rw = MLA_HEADS * MLA_ROPE
    return pl.pallas_call(
        _mla_prep_kernel,
        grid=(t // tm,),
        in_specs=[
            pl.BlockSpec((tm, MLA_KV_LORA), lambda i: (i, OD_CKV // MLA_KV_LORA)),
            pl.BlockSpec((tm, LANE), lambda i: (i, OD_KR // LANE)),
            pl.BlockSpec((tm, rw), lambda i: (i, QF_ROPE // rw)),
            pl.BlockSpec((tm, rw), lambda i: (i, QF_ROPE_SW // rw)),
            pl.BlockSpec((tm, LANE), lambda i: (i, 0)),
            pl.BlockSpec((tm, LANE), lambda i: (i, 0)),
            pl.BlockSpec((1, MLA_KV_LORA), lambda i: (0, 0)),
        ],
        out_specs=[pl.BlockSpec((tm, MLA_KV_LORA), lambda i: (i, 0)),
                   pl.BlockSpec((tm, MLA_ROPE), lambda i: (i, 0)),
                   pl.BlockSpec((tm, rw), lambda i: (i, 0))],
        out_shape=[jax.ShapeDtypeStruct((t, MLA_KV_LORA), F32),
                   jax.ShapeDtypeStruct((t, MLA_ROPE), F32),
                   jax.ShapeDtypeStruct((t, rw), F32)],
        compiler_params=_cparams("parallel"),
        name="mla_prep",
    )(proj, proj, qf, qf, cos, sin, kv_norm_g.reshape(1, MLA_KV_LORA))


def _mlap_kernel(qt_ref, kt_ref, qn_ref, qr_ref, kn_ref, kr_ref, v_ref, o_ref,
                 qc, kc, acc_ref, m_ref, l_ref, *, tb, scale):
    step = pl.program_id(1)
    qi = qt_ref[step]
    kj = kt_ref[step]
    hw = qc.shape[1] // MLA_HEADS

    @pl.when(kj == 0)
    def _():
        qc[...] = jnp.zeros_like(qc)
        kc[...] = jnp.zeros_like(kc)
        for h in range(MLA_HEADS):
            qc[:, h * hw:h * hw + MLA_NOPE] = qn_ref[:, h * MLA_NOPE:(h + 1) * MLA_NOPE].astype(BF16)
            qc[:, h * hw + MLA_NOPE:h * hw + MLA_NOPE + MLA_ROPE] = (
                qr_ref[:, h * MLA_ROPE:(h + 1) * MLA_ROPE].astype(BF16))
        acc_ref[...] = jnp.zeros_like(acc_ref)
        m_ref[...] = jnp.full_like(m_ref, NEG_BIG)
        l_ref[...] = jnp.zeros_like(l_ref)

    def sweep(diag):
        krb = kr_ref[...].astype(BF16)
        for h in range(MLA_HEADS):
            kc[:, h * hw:h * hw + MLA_NOPE] = kn_ref[:, h * MLA_NOPE:(h + 1) * MLA_NOPE].astype(BF16)
            kc[:, h * hw + MLA_NOPE:h * hw + MLA_NOPE + MLA_ROPE] = krb
        vb = v_ref[...].astype(BF16)
        if diag:
            row = lax.broadcasted_iota(jnp.int32, (tb, tb), 0)
            col = lax.broadcasted_iota(jnp.int32, (tb, tb), 1)
            mask = col <= row
        for h in range(MLA_HEADS):
            vs = slice(h * MLA_V, (h + 1) * MLA_V)
            s = _dot_nt(qc[:, h * hw:(h + 1) * hw], kc[:, h * hw:(h + 1) * hw]) * (scale * LOG2E)
            if diag:
                s = jnp.where(mask, s, NEG_BIG)
            m_prev = m_ref[h]
            m_new = jnp.maximum(m_prev, jnp.max(s, axis=-1, keepdims=True))
            alpha = jnp.exp2(m_prev - m_new)
            p = jnp.exp2(s - jnp.concatenate([m_new] * (tb // LANE), axis=1))
            if diag:
                p = jnp.where(mask, p, 0.0)
            l_new = alpha * l_ref[h]
            for c in range(tb // LANE):
                l_new = l_new + p[:, c * LANE:(c + 1) * LANE]
            l_ref[h] = l_new
            acc_ref[:, vs] = alpha * acc_ref[:, vs] + _dot(p.astype(BF16), vb[:, vs])
            m_ref[h] = m_new

    @pl.when(kj < qi)
    def _():
        sweep(False)

    @pl.when(kj == qi)
    def _():
        sweep(True)
        for h in range(MLA_HEADS):
            vs = slice(h * MLA_V, (h + 1) * MLA_V)
            o_ref[:, vs] = acc_ref[:, vs] / jnp.sum(l_ref[h], axis=-1, keepdims=True)


def mla_prompt(qf, qr, kv, kr, b, s):
    assert MLA_V == LANE and MLA_NOPE + MLA_ROPE <= 2 * MLA_NOPE
    tb = _pick(s, (256, 128))
    nq = s // tb
    qt, kt = _tri_schedule(nq, reverse=False)
    nw, rw, vw = MLA_HEADS * MLA_NOPE, MLA_HEADS * MLA_ROPE, MLA_HEADS * MLA_V
    grid_spec = pltpu.PrefetchScalarGridSpec(
        num_scalar_prefetch=2,
        grid=(b, int(qt.shape[0])),
        in_specs=[
            pl.BlockSpec((tb, nw), lambda bi, st, qt, kt: (bi * nq + qt[st], QF_NOPE // nw)),
            pl.BlockSpec((tb, rw), lambda bi, st, qt, kt: (bi * nq + qt[st], 0)),
            pl.BlockSpec((tb, nw), lambda bi, st, qt, kt: (bi * nq + kt[st], 0)),
            pl.BlockSpec((tb, MLA_ROPE), lambda bi, st, qt, kt: (bi * nq + kt[st], 0)),
            pl.BlockSpec((tb, vw), lambda bi, st, qt, kt: (bi * nq + kt[st], nw // vw)),
        ],
        out_specs=pl.BlockSpec((tb, vw), lambda bi, st, qt, kt: (bi * nq + qt[st], 0)),
        scratch_shapes=[pltpu.VMEM((tb, 2 * nw), BF16), pltpu.VMEM((tb, 2 * nw), BF16),
                        pltpu.VMEM((tb, vw), F32),
                        pltpu.VMEM((MLA_HEADS, tb, LANE), F32), pltpu.VMEM((MLA_HEADS, tb, LANE), F32)],
    )
    return pl.pallas_call(
        functools.partial(_mlap_kernel, tb=tb, scale=(MLA_NOPE + MLA_ROPE) ** -0.5),
        grid_spec=grid_spec,
        out_shape=jax.ShapeDtypeStruct((b * s, vw), F32),
        compiler_params=_cparams("parallel", "arbitrary"),
        name="mla_prompt",
    )(qt, kt, qf, qr, kv, kr, kv)


def _mlas_kernel(pt_ref, ql_ref, qr_ref, cn_ref, rn_ref, *rest, pg, scale, tokens):
    c_refs = rest[:pg]
    r_refs = rest[pg:2 * pg]
    o_ref, acc_ref, m_ref, l_ref = rest[2 * pg:]
    g = pl.program_id(1)
    rows, width = ql_ref.shape[0], cn_ref.shape[0]
    qlb = ql_ref[...].astype(BF16)
    qrb = qr_ref[...].astype(BF16)

    def partial_softmax(c_list, r_list, mask):
        cb = jnp.concatenate([c[...].astype(BF16) for c in c_list], axis=0)
        rb = jnp.concatenate([r[...].astype(BF16) for r in r_list], axis=1)
        s = (_dot_nt(qlb, cb) + _dot(qrb, rb)) * (scale * LOG2E)
        if mask is not None:
            s = jnp.where(mask, s, NEG_BIG)
        m = jnp.max(s, axis=-1, keepdims=True)
        p = jnp.exp2(s - m)
        if mask is not None:
            p = jnp.where(mask, p, 0.0)
        return m, jnp.sum(p, axis=-1, keepdims=True), _dot(p.astype(BF16), cb)

    def sweep(groups):
        parts = [partial_softmax(*grp) for grp in groups]
        m_prev = m_ref[...]
        m_new = m_prev
        for m, _, _ in parts:
            m_new = jnp.maximum(m_new, m)
        alpha = jnp.exp2(m_prev - m_new)
        l_new = alpha * l_ref[...]
        acc = alpha * acc_ref[...]
        for m, l_part, acc_part in parts:
            a = jnp.exp2(m - m_new)
            l_new = l_new + a * l_part
            acc = acc + a * acc_part
        l_ref[...] = l_new
        acc_ref[...] = acc
        m_ref[...] = m_new

    @pl.when(g == 0)
    def _():
        acc_ref[...] = jnp.zeros_like(acc_ref)
        m_ref[...] = jnp.full_like(m_ref, NEG_BIG)
        l_ref[...] = jnp.zeros_like(l_ref)
        row = lax.broadcasted_iota(jnp.int32, (rows, width), 0)
        col = lax.broadcasted_iota(jnp.int32, (rows, width), 1)
        sweep([([cn_ref], [rn_ref], col <= (row % tokens))])

    half = (pg + 1) // 2
    groups = [(c_refs[:half], r_refs[:half], None), (c_refs[half:], r_refs[half:], None)]
    sweep([grp for grp in groups if len(grp[0])])

    @pl.when(g == pl.num_programs(1) - 1)
    def _():
        o_ref[...] = acc_ref[...] / l_ref[...]


def mla_sample(q_lat, q_rope, ckv_new, kr_new, ckv_pool, kr_pool, page_table, tokens):
    n, rows, c = q_lat.shape
    r = q_rope.shape[2]
    page = ckv_pool.shape[1]
    n_pages = page_table.shape[1]
    pg = _pick(n_pages, (16, 8, 4, 2, 1))
    ng = n_pages // pg

    def page_map(i):
        return lambda ni, g, pt: (pt[ni * n_pages + g * pg + i], 0, 0)

    grid_spec = pltpu.PrefetchScalarGridSpec(
        num_scalar_prefetch=1,
        grid=(n, ng),
        in_specs=[
            pl.BlockSpec((None, rows, c), lambda ni, g, pt: (ni, 0, 0)),
            pl.BlockSpec((None, rows, r), lambda ni, g, pt: (ni, 0, 0)),
            pl.BlockSpec((None, page, c), lambda ni, g, pt: (ni, 0, 0)),
            pl.BlockSpec((None, r, page), lambda ni, g, pt: (ni, 0, 0)),
        ] + [pl.BlockSpec((None, page, c), page_map(i)) for i in range(pg)]
          + [pl.BlockSpec((None, r, page), page_map(i)) for i in range(pg)],
        out_specs=pl.BlockSpec((None, rows, c), lambda ni, g, pt: (ni, 0, 0)),
        scratch_shapes=[pltpu.VMEM((rows, c), F32), pltpu.VMEM((rows, 1), F32), pltpu.VMEM((rows, 1), F32)],
    )
    return pl.pallas_call(
        functools.partial(_mlas_kernel, pg=pg, scale=(MLA_NOPE + MLA_ROPE) ** -0.5, tokens=tokens),
        grid_spec=grid_spec,
        out_shape=jax.ShapeDtypeStruct((n, rows, c), F32),
        compiler_params=_cparams("parallel", "arbitrary"),
        name="mla_sample",
    )(page_table.reshape(-1), q_lat, q_rope, ckv_new, kr_new, *([ckv_pool] * pg), *([kr_pool] * pg))


def _swap_half(w):
    half = w.shape[-1] // 2
    return jnp.concatenate([-w[..., half:], w[..., :half]], axis=-1)


def _even_w_in(w):
    sizes = (SB_HEADS * SB_HEAD_DIM, SB_KV_HEADS * SB_HEAD_DIM, SB_KV_HEADS * SB_HEAD_DIM, 2 * CONV_CH)
    q, k, v, glu = jnp.split(w, np.cumsum(sizes)[:-1].tolist(), axis=1)
    return jnp.concatenate([q, glu, k, v], axis=1).astype(BF16)


def _odd_w_in(w):
    d = w.shape[0]
    sizes = (ML_HEADS * ML_DK, ML_HEADS * ML_DK, ML_HEADS * ML_DV, ML_HEADS, ML_HEADS, ML_HEADS * ML_DV,
             MLA_Q_LORA, MLA_KV_LORA, MLA_ROPE)
    mq, mk, mv, mi, mf, mo, cq, ckv, kr = jnp.split(w, np.cumsum(sizes)[:-1].tolist(), axis=1)
    pad = jnp.zeros((d, LANE - 2 * ML_HEADS), w.dtype)
    return jnp.concatenate([mq, mk, mv, mo, cq, ckv, kr, _swap_half(kr), mi, mf, pad], axis=1).astype(BF16)


def _w_uq(w):
    k = w.shape[0]
    w3 = w.reshape(k, MLA_HEADS, MLA_NOPE + MLA_ROPE)
    nope = w3[..., :MLA_NOPE].reshape(k, -1)
    rope = w3[..., MLA_NOPE:]
    return jnp.concatenate([nope, rope.reshape(k, -1), _swap_half(rope).reshape(k, -1)], axis=1).astype(BF16)


def _rope_tables(pos):
    half = MLA_ROPE // 2
    inv = ROPE_BASE ** (-jnp.arange(half, dtype=F32) / half)
    ang = pos.astype(F32)[:, None] * inv[None, :]
    reps = LANE // half
    return jnp.tile(jnp.cos(ang), (1, reps)), jnp.tile(jnp.sin(ang), (1, reps))


def _even_layer(hp, hs, mix_g, w_in, w_out, cw, cb, lg, lb, state_conv, k_pool, v_pool, page_table,
                bp, sp, bs, ss, page_size):
    hq = SB_HEADS * SB_HEAD_DIM
    hkv = SB_KV_HEADS * SB_HEAD_DIM
    w1, w2 = w_out[:hq], w_out[hq:]

    pp = norm_matmul(hp, mix_g, w_in)
    o_a = sb_prompt(pp, bp, sp)
    o_b, conv_p = conformer_conv(pp.reshape(bp, sp, EV_W), None, cw, cb, lg, lb)
    hp = out_proj(o_a, o_b.reshape(bp * sp, CONV_CH), w1, w2, hp)
    k_p = pp[:, EV_K:EV_K + hkv].reshape(bp, sp, SB_KV_HEADS, SB_HEAD_DIM)
    v_p = pp[:, EV_V:EV_V + hkv].reshape(bp, sp, SB_KV_HEADS, SB_HEAD_DIM)

    ps = norm_matmul(hs, mix_g, w_in)
    q = ps[:, :hq].reshape(bs, ss, SB_KV_HEADS, SB_GROUP, SB_HEAD_DIM)
    q = q.transpose(0, 2, 3, 1, 4).reshape(bs, SB_KV_HEADS * SB_GROUP * ss, SB_HEAD_DIM)
    k_s = ps[:, EV_K:EV_K + hkv].reshape(bs, ss, SB_KV_HEADS, SB_HEAD_DIM)
    v_s = ps[:, EV_V:EV_V + hkv].reshape(bs, ss, SB_KV_HEADS, SB_HEAD_DIM)
    pw = page_size * SB_KV_HEADS
    pad = ((0, 0), (0, pw - ss * SB_KV_HEADS), (0, 0))
    k_new = jnp.pad(k_s.reshape(bs, ss * SB_KV_HEADS, SB_HEAD_DIM), pad)
    v_new = jnp.pad(v_s.reshape(bs, ss * SB_KV_HEADS, SB_HEAD_DIM), pad)
    o = sb_sample(q, k_new, v_new, k_pool.reshape(-1, SB_HEAD_DIM), v_pool.reshape(-1, SB_HEAD_DIM),
                  page_table, page_size)
    o_a = o.reshape(bs, SB_KV_HEADS, SB_GROUP, ss, SB_HEAD_DIM).transpose(0, 3, 1, 2, 4).reshape(bs * ss, hq)
    o_b, conv_s = conformer_conv(ps.reshape(bs, ss, EV_W), state_conv, cw, cb, lg, lb)
    hs = out_proj(o_a, o_b.reshape(bs * ss, CONV_CH), w1, w2, hs)
    return hp, hs, (k_p, v_p, k_s, v_s, conv_p, conv_s)


def _gate_rows(proj3, L):
    b, t, _ = proj3.shape
    nc = -(-t // L)
    g = proj3[:, :, OD_G:OD_G + 2 * ML_HEADS].transpose(0, 2, 1)
    g = jnp.pad(g, ((0, 0), (0, 0), (0, nc * L - t)))
    return g.reshape(b, 2 * ML_HEADS, nc, L).transpose(0, 2, 1, 3)


def _odd_layer(hp, hs, mix_g, w_in, w_out, b_i, b_f, ml_norm_g, q_norm_g, kv_norm_g, w_uq, w_uk, w_uv,
               ml_state, ckv_pool, kr_pool, page_table, bp, sp, bs, ss, past_len):
    hv = ML_HEADS * ML_DV
    w1, w2 = w_out[:hv], w_out[hv:]
    gate_bias = jnp.concatenate([b_i, b_f]).astype(F32)
    c_lora = w_uk.shape[0]
    w_kv = jnp.concatenate([w_uk.reshape(c_lora, -1), w_uv.reshape(c_lora, -1)], axis=1).astype(BF16)
    w_q = _w_uq(w_uq)
    chunk = LANE

    pp = norm_matmul(hp, mix_g, w_in)
    pp3 = pp.reshape(bp, sp, OD_W)
    o_c, c_p, n_p, m_p = mlstm(pp3, _gate_rows(pp3, chunk), gate_bias, ml_norm_g, None, chunk)
    qf = norm_matmul(pp, q_norm_g, w_q, col_block=OD_CQ // MLA_Q_LORA)
    cos, sin = _rope_tables(jnp.tile(jnp.arange(sp), bp))
    ckv_p, kr_p, qr = mla_prep(pp, qf, cos, sin, kv_norm_g)
    kv = norm_matmul(pp, kv_norm_g, w_kv, col_block=OD_CKV // MLA_KV_LORA)
    o_d = mla_prompt(qf, qr, kv, kr_p, bp, sp)
    hp = out_proj(o_c.reshape(bp * sp, hv), o_d, w1, w2, hp)

    ps = norm_matmul(hs, mix_g, w_in)
    ps3 = ps.reshape(bs, ss, OD_W)
    o_c, c_s, n_s, m_s = mlstm(ps3, _gate_rows(ps3, chunk), gate_bias, ml_norm_g, ml_state, chunk)
    qf = norm_matmul(ps, q_norm_g, w_q, col_block=OD_CQ // MLA_Q_LORA)
    cos, sin = _rope_tables(jnp.tile(past_len + jnp.arange(ss), bs))
    ckv_s, kr_s, qr = mla_prep(ps, qf, cos, sin, kv_norm_g)
    q_nope = qf[:, :MLA_HEADS * MLA_NOPE].reshape(bs * ss, MLA_HEADS, MLA_NOPE).transpose(1, 0, 2)
    q_lat = head_matmul(q_nope, w_uk.transpose(1, 2, 0).astype(BF16))
    q_lat = q_lat.reshape(MLA_HEADS, bs, ss, c_lora).transpose(1, 0, 2, 3).reshape(bs, MLA_HEADS * ss, c_lora)
    q_rope = qr.reshape(bs, ss, MLA_HEADS, MLA_ROPE).transpose(0, 2, 1, 3).reshape(bs, MLA_HEADS * ss, MLA_ROPE)
    page = ckv_pool.shape[1]
    ckv_new = jnp.pad(ckv_s.reshape(bs, ss, c_lora), ((0, 0), (0, page - ss), (0, 0)))
    kr_new = jnp.pad(kr_s.reshape(bs, ss, MLA_ROPE).transpose(0, 2, 1), ((0, 0), (0, 0), (0, page - ss)))
    o_lat = mla_sample(q_lat, q_rope, ckv_new, kr_new, ckv_pool, kr_pool.transpose(0, 2, 1), page_table, ss)
    o_lat = o_lat.reshape(bs, MLA_HEADS, ss, c_lora).transpose(1, 0, 2, 3).reshape(MLA_HEADS, bs * ss, c_lora)
    o_d = head_matmul(o_lat, w_uv.transpose(1, 0, 2).astype(BF16))
    o_d = o_d.transpose(1, 0, 2).reshape(bs * ss, MLA_HEADS * MLA_V)
    hs = out_proj(o_c.reshape(bs * ss, hv), o_d, w1, w2, hs)

    outs = (c_p, n_p, m_p, c_s, n_s, m_s,
            ckv_p.reshape(bp, sp, c_lora), kr_p.reshape(bp, sp, MLA_ROPE),
            ckv_s.reshape(bs, ss, c_lora), kr_s.reshape(bs, ss, MLA_ROPE))
    return hp, hs, outs


def kernel(x_prompt, x_sample, cache_sb_k, cache_sb_v, state_conv, state_mlstm_C, state_mlstm_n, state_mlstm_m, cache_mla_ckv, cache_mla_krope, page_table, ffn_norm1_g, mix_norm_g, ffn_norm2_g, ffn1_w_gate, ffn1_w_up, ffn1_w_down, ffn2_w_gate, ffn2_w_up, ffn2_w_down, even_w_in, even_w_out, conv_w, conv_b, conv_ln_g, conv_ln_b, odd_w_in, odd_w_out, mlstm_b_i, mlstm_b_f, mlstm_norm_g, mla_q_norm_g, mla_kv_norm_g, mla_w_uq, mla_w_uk, mla_w_uv, final_norm_g):
    bp, sp, d = x_prompt.shape
    bs, ss, _ = x_sample.shape
    depth = ffn_norm1_g.shape[0]
    page_size = cache_sb_k.shape[2]
    past_len = page_table.shape[1] * page_size

    hp = x_prompt.reshape(bp * sp, d)
    hs = x_sample.reshape(bs * ss, d)
    even_outs, odd_outs = [], []
    for layer in range(depth):
        w = [a[layer].astype(BF16) for a in (ffn1_w_gate, ffn1_w_up, ffn1_w_down)]
        hp = ffn(hp, ffn_norm1_g[layer], *w)
        hs = ffn(hs, ffn_norm1_g[layer], *w)
        if layer % 2 == 0:
            e = layer // 2
            hp, hs, outs = _even_layer(
                hp, hs, mix_norm_g[layer], _even_w_in(even_w_in[e]), even_w_out[e].astype(BF16),
                conv_w[e], conv_b[e], conv_ln_g[e], conv_ln_b[e], state_conv[e],
                cache_sb_k[e], cache_sb_v[e], page_table, bp, sp, bs, ss, page_size)
            even_outs.append(outs)
        else:
            o = layer // 2
            hp, hs, outs = _odd_layer(
                hp, hs, mix_norm_g[layer], _odd_w_in(odd_w_in[o]), odd_w_out[o].astype(BF16),
                mlstm_b_i[o], mlstm_b_f[o], mlstm_norm_g[o], mla_q_norm_g[o], mla_kv_norm_g[o],
                mla_w_uq[o], mla_w_uk[o], mla_w_uv[o],
                (state_mlstm_C[o], state_mlstm_n[o], state_mlstm_m[o]),
                cache_mla_ckv[o], cache_mla_krope[o], page_table, bp, sp, bs, ss, past_len)
            odd_outs.append(outs)
        w = [a[layer].astype(BF16) for a in (ffn2_w_gate, ffn2_w_up, ffn2_w_down)]
        fg = final_norm_g if layer == depth - 1 else None
        hp = ffn(hp, ffn_norm2_g[layer], *w, final_g=fg)
        hs = ffn(hs, ffn_norm2_g[layer], *w, final_g=fg)

    y_prompt = hp.reshape(bp, sp, d)
    y_sample = hs.reshape(bs, ss, d)
    ev = [jnp.stack(x) for x in zip(*even_outs)]
    od = [jnp.stack(x) for x in zip(*odd_outs)]
    return (y_prompt, y_sample, *ev, *od)
```
